```python
import math
import jax
import jax.numpy as jnp
from jax import lax
import numpy as np

D_MODEL = 4096
BATCH = 8
SEQ = 2048
DEPTH = 4
DEC_BATCH = 8
DEC_SEQ = 16
PAST_LEN = 1024

CHUNK = 64
Q_BLOCK = 128
N_MIXERS = 4
ROPE_THETA = 10000.0
PLE_DIM = 256
LN_EPS = 1e-5
RMS_EPS = 1e-6
NEG_INF = -1e30
DN_ALPHA = (2 * DEPTH) ** 0.25
DN_BETA = (8 * DEPTH) ** -0.25

MLA_HEADS = 32
MLA_Q_RANK = 1024
MLA_KV_RANK = 512
MLA_NOPE = 128
MLA_ROPE = 64
MLA_V = 128

POOL_WINDOWS = (2, 4, 8, 16)
POOL_GROUP = D_MODEL // len(POOL_WINDOWS)
POOL_HIST = max(POOL_WINDOWS) - 1

CONV_WIDTH = 31
CONV_HIST = CONV_WIDTH - 1

DIFF_HEADS = 16
DIFF_HEAD_DIM = D_MODEL // (2 * DIFF_HEADS)

N_EXPERTS = 16
N_GROUPS = 4
EXPERTS_PER_GROUP = N_EXPERTS // N_GROUPS
TOP_K = 2
D_EXPERT = 1024

N_MLA = (DEPTH + 3) // 4
N_POOL = (DEPTH + 2) // 4
N_CONV = (DEPTH + 1) // 4
N_DIFF = DEPTH // 4

kernel_name = 'hybrid_streaming_encoder_step'


def layer_norm(x, g, b):
    xf = x.astype(jnp.float32)
    mu = jnp.mean(xf, -1, keepdims=True)
    var = jnp.mean(jnp.square(xf - mu), -1, keepdims=True)
    return ((xf - mu) * lax.rsqrt(var + LN_EPS) * g + b).astype(x.dtype)


def rms_norm(x, g):
    xf = x.astype(jnp.float32)
    return (xf * lax.rsqrt(jnp.mean(jnp.square(xf), -1, keepdims=True) + RMS_EPS) * g).astype(x.dtype)


def rope(x, pos):
    half = x.shape[-1] // 2
    inv_freq = ROPE_THETA ** (-jnp.arange(half, dtype=jnp.float32) / half)
    ang = pos.astype(jnp.float32)[:, None] * inv_freq[None, :]
    cos = jnp.cos(ang)[None, :, None, :]
    sin = jnp.sin(ang)[None, :, None, :]
    x1 = x[..., :half].astype(jnp.float32)
    x2 = x[..., half:].astype(jnp.float32)
    return jnp.concatenate([x1 * cos - x2 * sin, x2 * cos + x1 * sin], -1).astype(x.dtype)


def chunk_mask(q_pos, k_pos):
    return (k_pos[None, :] // CHUNK) <= (q_pos[:, None] // CHUNK)


def masked_softmax(s, mask):
    return jax.nn.softmax(jnp.where(mask, s, NEG_INF), axis=-1)


def sweep_query_blocks(attend, qs, q_pos):
    sq = q_pos.shape[0]
    if sq <= Q_BLOCK or sq % Q_BLOCK:
        return attend(qs, q_pos)
    nb = sq // Q_BLOCK
    qs_b = tuple(jnp.moveaxis(q.reshape(q.shape[0], nb, Q_BLOCK, *q.shape[2:]), 1, 0) for q in qs)
    out = lax.map(lambda a: attend(a[0], a[1]), (qs_b, q_pos.reshape(nb, Q_BLOCK)))
    out = jnp.moveaxis(out, 0, 1)
    return out.reshape(out.shape[0], sq, *out.shape[3:])


def mla_mixer(x, pos, k_pos, past_c, past_pe, w_dq, g_q, w_uq, w_dkv, g_kv, w_ukv, w_o):
    b, s, _ = x.shape
    q = (rms_norm(x @ w_dq, g_q) @ w_uq).reshape(b, s, MLA_HEADS, MLA_NOPE + MLA_ROPE)
    q_nope = q[..., :MLA_NOPE]
    q_pe = rope(q[..., MLA_NOPE:], pos)
    kv = x @ w_dkv
    c_new = rms_norm(kv[..., :MLA_KV_RANK], g_kv)
    pe_new = rope(kv[..., None, MLA_KV_RANK:], pos)[:, :, 0]
    c_all = c_new if past_c is None else jnp.concatenate([past_c, c_new], axis=1)
    pe_all = pe_new if past_pe is None else jnp.concatenate([past_pe, pe_new], axis=1)
    kv_up = (c_all @ w_ukv).reshape(b, -1, MLA_HEADS, MLA_NOPE + MLA_V)
    k_nope = kv_up[..., :MLA_NOPE]
    v = kv_up[..., MLA_NOPE:]
    scale = (MLA_NOPE + MLA_ROPE) ** -0.5

    def attend(qs, qp):
        qn, qr = qs
        sc = (jnp.einsum('bqhd,bkhd->bhqk', qn, k_nope)
              + jnp.einsum('bqhr,bkr->bhqk', qr, pe_all)).astype(jnp.float32) * scale
        a = masked_softmax(sc, chunk_mask(qp, k_pos)[None, None]).astype(v.dtype)
        return jnp.einsum('bhqk,bkhd->bqhd', a, v)

    o = sweep_query_blocks(attend, (q_nope, q_pe), pos)
    return o.reshape(b, s, MLA_HEADS * MLA_V) @ w_o, c_new, pe_new


def pool_mixer(x, hist, w_pool, scale):
    b, s, _ = x.shape
    n_hist = 0 if hist is None else hist.shape[1]
    u = x if hist is None else jnp.concatenate([hist, x], axis=1)
    cs = jnp.concatenate([jnp.zeros((b, 1, D_MODEL), jnp.float32),
                          jnp.cumsum(u.astype(jnp.float32), axis=1)], axis=1)
    end = n_hist + 1 + jnp.arange(s)
    outs = []
    for g, win in enumerate(POOL_WINDOWS):
        lo, hi = g * POOL_GROUP, (g + 1) * POOL_GROUP
        start = jnp.maximum(end - win, 0)
        mean = (cs[:, end, lo:hi] - cs[:, start, lo:hi]) / (end - start).astype(jnp.float32)[None, :, None]
        outs.append((mean - x[..., lo:hi].astype(jnp.float32)).astype(x.dtype) @ w_pool[g])
    y = jnp.concatenate(outs, axis=-1) * scale
    return y, u[:, -POOL_HIST:]


def conv_mixer(x, hist, w_pw1, b_pw1, w_dw, b_dw, ln_g, ln_b, w_pw2, b_pw2):
    b, s, _ = x.shape
    h = x @ w_pw1 + b_pw1
    u = h[..., :D_MODEL] * jax.nn.sigmoid(h[..., D_MODEL:])
    if hist is None:
        hist = jnp.zeros((b, CONV_HIST, D_MODEL), u.dtype)
    ua = jnp.concatenate([hist, u], axis=1)
    c = ua[:, :s] * w_dw[0]
    for k in range(1, CONV_WIDTH):
        c = c + ua[:, k:k + s] * w_dw[k]
    c = layer_norm(c + b_dw, ln_g, ln_b)
    c = c * jax.nn.sigmoid(c)
    return c @ w_pw2 + b_pw2, ua[:, -CONV_HIST:]


def diff_mixer(x, pos, k_pos, past_k, past_v, w_qkv, lam_q1, lam_k1, lam_q2, lam_k2, g_sub, w_o, lam_init):
    b, s, _ = x.shape
    f32 = jnp.float32
    qkv = x @ w_qkv
    q = rope(qkv[..., :D_MODEL].reshape(b, s, 2 * DIFF_HEADS, DIFF_HEAD_DIM), pos
             ).reshape(b, s, DIFF_HEADS, 2, DIFF_HEAD_DIM)
    k_new = rope(qkv[..., D_MODEL:2 * D_MODEL].reshape(b, s, 2 * DIFF_HEADS, DIFF_HEAD_DIM), pos
                 ).reshape(b, s, DIFF_HEADS, 2, DIFF_HEAD_DIM)
    v_new = qkv[..., 2 * D_MODEL:].reshape(b, s, DIFF_HEADS, 2 * DIFF_HEAD_DIM)
    k = k_new if past_k is None else jnp.concatenate([past_k, k_new], axis=1)
    v = v_new if past_v is None else jnp.concatenate([past_v, v_new], axis=1)
    lam = (jnp.exp(jnp.sum(lam_q1.astype(f32) * lam_k1.astype(f32)))
           - jnp.exp(jnp.sum(lam_q2.astype(f32) * lam_k2.astype(f32))) + lam_init)
    scale = DIFF_HEAD_DIM ** -0.5

    def attend(qs, qp):
        (qb,) = qs
        sc = jnp.einsum('bqhcd,bkhcd->bhcqk', qb, k).astype(f32) * scale
        pr = masked_softmax(sc, chunk_mask(qp, k_pos)[None, None, None])
        a = (pr[:, :, 0] - lam * pr[:, :, 1]).astype(v.dtype)
        return jnp.einsum('bhqk,bkhe->bqhe', a, v)

    o = sweep_query_blocks(attend, (q,), pos)
    o = rms_norm(o, g_sub) * (1.0 - lam_init)
    return o.reshape(b, s, D_MODEL) @ w_o, k_new, v_new


def grouped_moe(x, router_w, router_b, w1, w3, w2):
    b, s, _ = x.shape
    t = x.reshape(b * s, D_MODEL)
    scores = jax.nn.sigmoid((t @ router_w).astype(jnp.float32))
    biased = (scores + router_b).reshape(-1, N_GROUPS, EXPERTS_PER_GROUP)
    group_score = jnp.sum(lax.top_k(biased, TOP_K)[0], axis=-1)
    g_sel = jnp.argmax(group_score, axis=-1)
    in_group = jnp.take_along_axis(biased, g_sel[:, None, None], axis=1)[:, 0]
    _, local = lax.top_k(in_group, TOP_K)
    e_idx = g_sel[:, None] * EXPERTS_PER_GROUP + local
    w = jnp.take_along_axis(scores, e_idx, axis=1)
    w = w / jnp.sum(w, axis=-1, keepdims=True)
    gates = jnp.sum(jax.nn.one_hot(e_idx, N_EXPERTS, dtype=jnp.float32) * w[..., None], axis=1)
    y = jnp.zeros((t.shape[0], D_MODEL), jnp.float32)
    for e in range(N_EXPERTS):
        h = jax.nn.silu(t @ w1[e]) * (t @ w3[e])
        y = y + gates[:, e:e + 1] * (h @ w2[e]).astype(jnp.float32)
    return y.astype(x.dtype).reshape(b, s, D_MODEL)


def trunk(x, p, past, prm):
    b, s, _ = x.shape
    n_past = 0 if past is None else past['cache_mla_ckv'].shape[2]
    pos = n_past + jnp.arange(s)
    k_pos = jnp.arange(n_past + s)
    new = {'mla_ckv': [], 'mla_kpe': [], 'pool': [], 'conv': [], 'diff_k': [], 'diff_v': []}
    for i in range(DEPTH):
        kind, j = i % N_MIXERS, i // N_MIXERS
        if kind == 0:
            h, c_new, pe_new = mla_mixer(
                x, pos, k_pos,
                None if past is None else past['cache_mla_ckv'][j],
                None if past is None else past['cache_mla_kpe'][j],
                prm['mla_w_dq'][j], prm['mla_g_q'][j], prm['mla_w_uq'][j], prm['mla_w_dkv'][j],
                prm['mla_g_kv'][j], prm['mla_w_ukv'][j], prm['mla_w_o'][j])
            new['mla_ckv'].append(c_new)
            new['mla_kpe'].append(pe_new)
        elif kind == 1:
            h, hist = pool_mixer(x, None if past is None else past['state_pool'][j],
                                 prm['pool_w'][j], prm['pool_scale'][j])
            new['pool'].append(hist)
        elif kind == 2:
            h, hist = conv_mixer(x, None if past is None else past['state_conv'][j],
                                 prm['conv_w_pw1'][j], prm['conv_b_pw1'][j], prm['conv_w_dw'][j],
                                 prm['conv_b_dw'][j], prm['conv_ln_g'][j], prm['conv_ln_b'][j],
                                 prm['conv_w_pw2'][j], prm['conv_b_pw2'][j])
            new['conv'].append(hist)
        else:
            lam_init = 0.8 - 0.6 * math.exp(-0.3 * i)
            h, k_new, v_new = diff_mixer(
                x, pos, k_pos,
                None if past is None else past['cache_diff_k'][j],
                None if past is None else past['cache_diff_v'][j],
                prm['diff_w_qkv'][j], prm['diff_lam_q1'][j], prm['diff_lam_k1'][j],
                prm['diff_lam_q2'][j], prm['diff_lam_k2'][j], prm['diff_g_sub'][j],
                prm['diff_w_o'][j], lam_init)
            new['diff_k'].append(k_new)
            new['diff_v'].append(v_new)
        x = layer_norm(DN_ALPHA * x + h, prm['ln1_g'][i], prm['ln1_b'][i])
        f = grouped_moe(x, prm['router_w'], prm['router_b'], prm['moe_w1'][i], prm['moe_w3'][i], prm['moe_w2'][i])
        x = layer_norm(DN_ALPHA * x + f, prm['ln2_g'][i], prm['ln2_b'][i])
        x = x + (p[i] @ prm['ple_w'][i]) * jax.nn.sigmoid(x @ prm['ple_gate'][i])
    return x, new


def setup_inputs(seed: int = 0) -> dict:
    key = jax.random.key(seed)
    keys = iter(jax.random.split(key, 64))

    def nrm(shape, scale=1.0):
        return jax.random.normal(next(keys), shape, jnp.float32) * scale

    def gain(shape):
        return 1.0 + nrm(shape, 0.02)

    H, d = DIFF_HEADS, DIFF_HEAD_DIM
    return {
        'x_prompt': nrm((BATCH, SEQ, D_MODEL)),
        'x_sample': nrm((DEC_BATCH, DEC_SEQ, D_MODEL)),
        'cache_mla_ckv': nrm((N_MLA, DEC_BATCH, PAST_LEN, MLA_KV_RANK)),
        'cache_mla_kpe': nrm((N_MLA, DEC_BATCH, PAST_LEN, MLA_ROPE)),
        'state_pool': nrm((N_POOL, DEC_BATCH, POOL_HIST, D_MODEL)),
        'state_conv': nrm((N_CONV, DEC_BATCH, CONV_HIST, D_MODEL), 0.5),
        'cache_diff_k': nrm((N_DIFF, DEC_BATCH, PAST_LEN, H, 2, d)),
        'cache_diff_v': nrm((N_DIFF, DEC_BATCH, PAST_LEN, H, 2 * d)),
        'p_prompt': nrm((DEPTH, BATCH, SEQ, PLE_DIM)),
        'p_sample': nrm((DEPTH, DEC_BATCH, DEC_SEQ, PLE_DIM)),
        'mla_w_dq': nrm((N_MLA, D_MODEL, MLA_Q_RANK), D_MODEL ** -0.5),
        'mla_g_q': gain((N_MLA, MLA_Q_RANK)),
        'mla_w_uq': nrm((N_MLA, MLA_Q_RANK, MLA_HEADS * (MLA_NOPE + MLA_ROPE)), MLA_Q_RANK ** -0.5),
        'mla_w_dkv': nrm((N_MLA, D_MODEL, MLA_KV_RANK + MLA_ROPE), D_MODEL ** -0.5),
        'mla_g_kv': gain((N_MLA, MLA_KV_RANK)),
        'mla_w_ukv': nrm((N_MLA, MLA_KV_RANK, MLA_HEADS * (MLA_NOPE + MLA_V)), MLA_KV_RANK ** -0.5),
        'mla_w_o': nrm((N_MLA, MLA_HEADS * MLA_V, D_MODEL), DN_BETA * (MLA_HEADS * MLA_V) ** -0.5),
        'pool_w': nrm((N_POOL, len(POOL_WINDOWS), POOL_GROUP, POOL_GROUP), DN_BETA * POOL_GROUP ** -0.5),
        'pool_scale': 1.0 + nrm((N_POOL, D_MODEL), 0.1),
        'conv_w_pw1': nrm((N_CONV, D_MODEL, 2 * D_MODEL), D_MODEL ** -0.5),
        'conv_b_pw1': nrm((N_CONV, 2 * D_MODEL), 0.01),
        'conv_w_dw': nrm((N_CONV, CONV_WIDTH, D_MODEL), CONV_WIDTH ** -0.5),
        'conv_b_dw': nrm((N_CONV, D_MODEL), 0.01),
        'conv_ln_g': gain((N_CONV, D_MODEL)),
        'conv_ln_b': nrm((N_CONV, D_MODEL), 0.01),
        'conv_w_pw2': nrm((N_CONV, D_MODEL, D_MODEL), DN_BETA * D_MODEL ** -0.5),
        'conv_b_pw2': nrm((N_CONV, D_MODEL), 0.01),
        'diff_w_qkv': nrm((N_DIFF, D_MODEL, 3 * D_MODEL), D_MODEL ** -0.5),
        'diff_lam_q1': nrm((N_DIFF, d), 0.1),
        'diff_lam_k1': nrm((N_DIFF, d), 0.1),
        'diff_lam_q2': nrm((N_DIFF, d), 0.1),
        'diff_lam_k2': nrm((N_DIFF, d), 0.1),
        'diff_g_sub': gain((N_DIFF, 2 * d)),
        'diff_w_o': nrm((N_DIFF, D_MODEL, D_MODEL), DN_BETA * D_MODEL ** -0.5),
        'router_w': nrm((D_MODEL, N_EXPERTS), D_MODEL ** -0.5),
        'router_b': nrm((N_EXPERTS,), 0.01),
        'moe_w1': nrm((DEPTH, N_EXPERTS, D_MODEL, D_EXPERT), D_MODEL ** -0.5),
        'moe_w3': nrm((DEPTH, N_EXPERTS, D_MODEL, D_EXPERT), D_MODEL ** -0.5),
        'moe_w2': nrm((DEPTH, N_EXPERTS, D_EXPERT, D_MODEL), DN_BETA * D_EXPERT ** -0.5),
        'ln1_g': gain((DEPTH, D_MODEL)),
        'ln1_b': nrm((DEPTH, D_MODEL), 0.01),
        'ln2_g': gain((DEPTH, D_MODEL)),
        'ln2_b': nrm((DEPTH, D_MODEL), 0.01),
        'ple_w': nrm((DEPTH, PLE_DIM, D_MODEL), 0.5 * PLE_DIM ** -0.5),
        'ple_gate': nrm((DEPTH, D_MODEL, D_MODEL), D_MODEL ** -0.5),
    }


def reference(x_prompt, x_sample, cache_mla_ckv, cache_mla_kpe, state_pool, state_conv, cache_diff_k,
              cache_diff_v, p_prompt, p_sample, mla_w_dq, mla_g_q, mla_w_uq, mla_w_dkv, mla_g_kv, mla_w_ukv,
              mla_w_o, pool_w, pool_scale, conv_w_pw1, conv_b_pw1, conv_w_dw, conv_b_dw, conv_ln_g, conv_ln_b,
              conv_w_pw2, conv_b_pw2, diff_w_qkv, diff_lam_q1, diff_lam_k1, diff_lam_q2, diff_lam_k2,
              diff_g_sub, diff_w_o, router_w, router_b, moe_w1, moe_w3, moe_w2, ln1_g, ln1_b, ln2_g, ln2_b,
              ple_w, ple_gate):
    prm = {
        'mla_w_dq': mla_w_dq, 'mla_g_q': mla_g_q, 'mla_w_uq': mla_w_uq, 'mla_w_dkv': mla_w_dkv,
        'mla_g_kv': mla_g_kv, 'mla_w_ukv': mla_w_ukv, 'mla_w_o': mla_w_o,
        'pool_w': pool_w, 'pool_scale': pool_scale,
        'conv_w_pw1': conv_w_pw1, 'conv_b_pw1': conv_b_pw1, 'conv_w_dw': conv_w_dw, 'conv_b_dw': conv_b_dw,
        'conv_ln_g': conv_ln_g, 'conv_ln_b': conv_ln_b, 'conv_w_pw2': conv_w_pw2, 'conv_b_pw2': conv_b_pw2,
        'diff_w_qkv': diff_w_qkv, 'diff_lam_q1': diff_lam_q1, 'diff_lam_k1': diff_lam_k1,
        'diff_lam_q2': diff_lam_q2, 'diff_lam_k2': diff_lam_k2, 'diff_g_sub': diff_g_sub, 'diff_w_o': diff_w_o,
        'router_w': router_w, 'router_b': router_b, 'moe_w1': moe_w1, 'moe_w3': moe_w3, 'moe_w2': moe_w2,
        'ln1_g': ln1_g, 'ln1_b': ln1_b, 'ln2_g': ln2_g, 'ln2_b': ln2_b, 'ple_w': ple_w, 'ple_gate': ple_gate,
    }
    past = {
        'cache_mla_ckv': cache_mla_ckv, 'cache_mla_kpe': cache_mla_kpe, 'state_pool': state_pool,
        'state_conv': state_conv, 'cache_diff_k': cache_diff_k, 'cache_diff_v': cache_diff_v,
    }
    y_prompt, sp = trunk(x_prompt, p_prompt, None, prm)
    y_sample, ss = trunk(x_sample, p_sample, past, prm)
    return (y_prompt, y_sample,
            jnp.stack(sp['mla_ckv']), jnp.stack(sp['mla_kpe']), jnp.stack(sp['pool']),
            jnp.stack(sp['conv']), jnp.stack(sp['diff_k']), jnp.stack(sp['diff_v']),
            jnp.stack(ss['mla_ckv']), jnp.stack(ss['mla_kpe']), jnp.stack(ss['pool']),
            jnp.stack(ss['conv']), jnp.stack(ss['diff_k']), jnp.stack(ss['diff_v']))
```

```python
import functools
import math

import jax
import jax.numpy as jnp
from jax import lax
from jax.experimental import pallas as pl
from jax.experimental.pallas import tpu as pltpu

D_MODEL = 4096
BATCH = 8
SEQ = 2048
DEPTH = 4
DEC_BATCH = 8
DEC_SEQ = 16
PAST_LEN = 1024

CHUNK = 64
N_MIXERS = 4
ROPE_THETA = 10000.0
PLE_DIM = 256
LN_EPS = 1e-5
RMS_EPS = 1e-6
NEG_INF = -1e30
DN_ALPHA = (2 * DEPTH) ** 0.25

MLA_HEADS = 32
MLA_Q_RANK = 1024
MLA_KV_RANK = 512
MLA_NOPE = 128
MLA_ROPE = 64
MLA_V = 128

POOL_WINDOWS = (2, 4, 8, 16)
CONV_WIDTH = 31

DIFF_HEADS = 16

N_EXPERTS = 16
N_GROUPS = 4
TOP_K = 2
D_EXPERT = 1024

LANES = 128
MLA_HEAD_PAD = 2 * LANES
VMEM_LIMIT = 56 * 1024 * 1024
VMEM_BUDGET = 44 * 1024 * 1024
ROW_PAD = 256
MOE_TM = 256
SEQ_TQ = 512
CONV_TM = 64
POOL_TM = 256

BF16 = jnp.bfloat16
F32 = jnp.float32


def _round_up(n, m):
    return (n + m - 1) // m * m


def _pick(n, cands):
    for c in cands:
        if c <= n and n % c == 0:
            return c
    return n


def _params(sem):
    return pltpu.CompilerParams(dimension_semantics=sem, vmem_limit_bytes=VMEM_LIMIT)


def _fused_matmul(x, w, epilogue, *, tn, n_cols, col0=0, w_lead=(), aux=(), outs=(), x_kblock=None,
                  w_lead_from_j=False, name="mm"):
    M = x.shape[0]
    K = w.shape[-2]
    n_j = n_cols // tn
    assert n_cols % tn == 0 and col0 % tn == 0
    cb = col0 // tn
    nl = len(w.shape) - 2

    def vmem_bytes(tm):
        b = 2 * tm * K * 2 + 2 * K * tn * 4 + K * tn * 2 + tm * tn * 4
        for a in aux:
            if a[1] == 'row':
                b += 2 * tm * a[0].shape[1] * a[0].dtype.itemsize
            elif a[1] == 'tile':
                b += 2 * tm * tn * a[0].dtype.itemsize
            elif a[1] == 'colw':
                b += 2 * a[0].shape[-2] * tn * a[0].dtype.itemsize
        for width, dt in outs:
            b += 2 * tm * width * jnp.dtype(dt).itemsize
        return b

    tm = M
    for c in (1280, 1024, 768, 640, 512, 384, 256, 128, 64, 32, 16):
        if c <= M and M % c == 0 and vmem_bytes(c) <= VMEM_BUDGET:
            tm = c
            break
    n_i = M // tm

    if x_kblock is None:
        x_spec = pl.BlockSpec((tm, K), lambda j, i: (i, 0))
    else:
        x_spec = pl.BlockSpec((tm, K), lambda j, i: (i, j))
    if w_lead_from_j:
        w_spec = pl.BlockSpec((None,) * nl + (K, tn), lambda j, i: (j,) * nl + (0, 0))
    else:
        w_spec = pl.BlockSpec((None,) * nl + (K, tn), lambda j, i: tuple(w_lead) + (0, cb + j))
    in_specs = [x_spec, w_spec]
    operands = [x, w]
    for a in aux:
        arr, kind = a[0], a[1]
        if kind == 'row':
            in_specs.append(pl.BlockSpec((tm, arr.shape[1]), lambda j, i: (i, 0)))
        elif kind == 'tile':
            in_specs.append(pl.BlockSpec((tm, tn), lambda j, i: (i, j)))
        elif kind == 'col':
            in_specs.append(pl.BlockSpec((1, tn), lambda j, i: (0, cb + j)))
        elif kind == 'full':
            in_specs.append(pl.BlockSpec(arr.shape, lambda j, i: (0, 0)))
        elif kind == 'colw':
            lead2 = tuple(a[2])
            ka = arr.shape[-2]
            in_specs.append(pl.BlockSpec((None,) * len(lead2) + (ka, tn),
                                         lambda j, i, lead2=lead2: lead2 + (0, cb + j)))
        else:
            raise ValueError(kind)
        operands.append(arr)
    out_specs = [pl.BlockSpec((tm, width), lambda j, i: (i, j)) for width, _ in outs]
    out_shape = [jax.ShapeDtypeStruct((M, width * n_j), dt) for width, dt in outs]
    na, no = len(aux), len(outs)

    def body(x_ref, w_ref, *rest):
        aux_refs = rest[:na]
        out_refs = rest[na:na + no]
        wb_ref = rest[na + no]

        @pl.when(pl.program_id(1) == 0)
        def _():
            wb_ref[...] = w_ref[...].astype(BF16)

        acc = jnp.dot(x_ref[...], wb_ref[...], preferred_element_type=F32)
        epilogue(acc, aux_refs, out_refs)

    res = pl.pallas_call(
        body,
        grid=(n_j, n_i),
        in_specs=in_specs,
        out_specs=out_specs,
        out_shape=out_shape,
        scratch_shapes=[pltpu.VMEM((K, tn), BF16)],
        compiler_params=_params(("arbitrary", "arbitrary")),
        name=name,
    )(*operands)
    return res


def _rope_slab(s, c, sn, half):
    if 2 * half == LANES:
        partner = pltpu.roll(s, half, axis=1)
    else:
        lane = lax.broadcasted_iota(jnp.int32, s.shape, 1)
        partner = jnp.where(lane < half, pltpu.roll(s, LANES - half, axis=1), pltpu.roll(s, half, axis=1))
    return s * c + partner * sn


def _ep_plain(acc, aux_refs, out_refs):
    for o in out_refs:
        o[...] = acc.astype(o.dtype)


def _ep_bias(acc, aux_refs, out_refs):
    out_refs[0][...] = (acc + aux_refs[0][...]).astype(out_refs[0].dtype)


def _ep_scale_cols(acc, aux_refs, out_refs):
    out_refs[0][...] = (acc * aux_refs[0][...]).astype(out_refs[0].dtype)


def _ep_rms(acc, aux_refs, out_refs):
    g = aux_refs[0][...]
    y = acc * lax.rsqrt(jnp.mean(acc * acc, axis=-1, keepdims=True) + RMS_EPS) * g
    out_refs[0][...] = y.astype(out_refs[0].dtype)


def _ep_mla_dkv(acc, aux_refs, out_refs):
    g_ref, c_ref, s_ref = aux_refs
    cf_ref, cb_ref, pf_ref, pb_ref = out_refs
    r = MLA_KV_RANK
    lat = acc[:, :r]
    cn = lat * lax.rsqrt(jnp.mean(lat * lat, axis=-1, keepdims=True) + RMS_EPS) * g_ref[...]
    cf_ref[...] = cn
    cb_ref[...] = cn.astype(BF16)
    pe = _rope_slab(acc[:, r:r + LANES], c_ref[...], s_ref[...], MLA_ROPE // 2)
    pf_ref[...] = pe
    pb_ref[...] = pe.astype(BF16)


def _ep_mla_q(acc, aux_refs, out_refs, *, scale, heads_per_tile):
    c = aux_refs[0][...]
    sn = aux_refs[1][...]
    o = out_refs[0]
    for h in range(heads_per_tile):
        a = h * MLA_HEAD_PAD
        o[:, a:a + LANES] = (acc[:, a:a + LANES] * scale).astype(BF16)
        pe = _rope_slab(acc[:, a + LANES:a + 2 * LANES], c, sn, MLA_ROPE // 2)
        o[:, a + LANES:a + 2 * LANES] = (pe * scale).astype(BF16)


def _ep_mla_k(acc, aux_refs, out_refs, *, heads_per_tile):
    pe = aux_refs[0][...]
    o = out_refs[0]
    for h in range(heads_per_tile):
        o[:, h * MLA_HEAD_PAD:h * MLA_HEAD_PAD + LANES] = acc[:, h * LANES:(h + 1) * LANES].astype(BF16)
        o[:, h * MLA_HEAD_PAD + LANES:(h + 1) * MLA_HEAD_PAD] = pe


def _ep_rope_heads(acc, aux_refs, out_refs, *, scale, n_slabs, want_f32):
    c = aux_refs[0][...]
    sn = aux_refs[1][...]
    for h in range(n_slabs):
        sl = slice(h * LANES, (h + 1) * LANES)
        y = _rope_slab(acc[:, sl], c, sn, LANES // 2)
        if want_f32:
            out_refs[0][:, sl] = y
            out_refs[1][:, sl] = y.astype(BF16)
        else:
            out_refs[0][:, sl] = (y * scale).astype(BF16)


def _ep_ple(acc, aux_refs, out_refs):
    x2_ref, p_ref, pw_ref = aux_refs
    p = p_ref[...]
    w = pw_ref[...]
    p_hi = p.astype(BF16)
    p_lo = (p - p_hi.astype(F32)).astype(BF16)
    w_hi = w.astype(BF16)
    w_lo = (w - w_hi.astype(F32)).astype(BF16)
    pw = (jnp.dot(p_hi, w_hi, preferred_element_type=F32) + jnp.dot(p_lo, w_hi, preferred_element_type=F32)
          + jnp.dot(p_hi, w_lo, preferred_element_type=F32))
    y = x2_ref[...] + pw * jax.nn.sigmoid(acc)
    out_refs[0][...] = y
    out_refs[1][...] = y.astype(BF16)


def _glu_matmul(x, w, b, *, name="glu"):
    M, K = x.shape
    D = w.shape[1] // 2
    tn = _pick(D, (512, 256, 128))
    n_j = D // tn
    tm = _pick(M, (1280, 1024, 640, 512, 256, 128, 64, 32, 16))
    while 2 * tm * K * 2 + 4 * K * tn * 4 + 2 * K * tn * 2 + 4 * tm * tn * 4 > VMEM_BUDGET and tm % 2 == 0 \
            and M % (tm // 2) == 0:
        tm //= 2
    n_i = M // tm

    def body(x_ref, wa_ref, wg_ref, ba_ref, bg_ref, o_ref, wab_ref, wgb_ref):
        @pl.when(pl.program_id(1) == 0)
        def _():
            wab_ref[...] = wa_ref[...].astype(BF16)
            wgb_ref[...] = wg_ref[...].astype(BF16)

        xv = x_ref[...]
        a = jnp.dot(xv, wab_ref[...], preferred_element_type=F32) + ba_ref[...]
        g = jnp.dot(xv, wgb_ref[...], preferred_element_type=F32) + bg_ref[...]
        o_ref[...] = a * jax.nn.sigmoid(g)

    return pl.pallas_call(
        body,
        grid=(n_j, n_i),
        in_specs=[pl.BlockSpec((tm, K), lambda j, i: (i, 0)),
                  pl.BlockSpec((K, tn), lambda j, i: (0, j)),
                  pl.BlockSpec((K, tn), lambda j, i: (0, n_j + j)),
                  pl.BlockSpec((1, tn), lambda j, i: (0, j)),
                  pl.BlockSpec((1, tn), lambda j, i: (0, n_j + j))],
        out_specs=pl.BlockSpec((tm, tn), lambda j, i: (i, j)),
        out_shape=jax.ShapeDtypeStruct((M, D), F32),
        scratch_shapes=[pltpu.VMEM((K, tn), BF16), pltpu.VMEM((K, tn), BF16)],
        compiler_params=_params(("arbitrary", "arbitrary")),
        name=name,
    )(x, w, w, b, b)


def _route_rows(scores, biased):
    epg = N_EXPERTS // N_GROUPS
    one = jnp.float32(1.0)
    zero = jnp.float32(0.0)
    rows = [biased[e:e + 1, :] for e in range(N_EXPERTS)]
    sel_in = []
    gscore = []
    for g in range(N_GROUPS):
        v = rows[g * epg:(g + 1) * epg]
        gs = None
        for k in range(epg):
            rank = None
            for m in range(epg):
                if m == k:
                    continue
                ahead = (v[m] > v[k]) if m > k else (v[m] >= v[k])
                t = jnp.where(ahead, one, zero)
                rank = t if rank is None else rank + t
            s = jnp.where(rank < TOP_K, one, zero)
            sel_in.append(s)
            contrib = s * v[k]
            gs = contrib if gs is None else gs + contrib
        gscore.append(gs)
    gsel = []
    for g in range(N_GROUPS):
        rank = None
        for m in range(N_GROUPS):
            if m == g:
                continue
            ahead = (gscore[m] > gscore[g]) if m > g else (gscore[m] >= gscore[g])
            t = jnp.where(ahead, one, zero)
            rank = t if rank is None else rank + t
        gsel.append(jnp.where(rank < 1, one, zero))
    sel = [sel_in[e] * gsel[e // epg] for e in range(N_EXPERTS)]
    denom = None
    for e in range(N_EXPERTS):
        t = sel[e] * scores[e:e + 1, :]
        denom = t if denom is None else denom + t
    gates = [sel[e] * scores[e:e + 1, :] / denom for e in range(N_EXPERTS)]
    return jnp.concatenate(gates, axis=0), jnp.concatenate(sel, axis=0)


def _ln_body(x_ref, h_ref, g_ref, b_ref, *rest, route):
    x = DN_ALPHA * x_ref[...] + h_ref[...]
    mu = jnp.mean(x, axis=-1, keepdims=True)
    xc = x - mu
    var = jnp.mean(xc * xc, axis=-1, keepdims=True)
    y = xc * lax.rsqrt(var + LN_EPS) * g_ref[...] + b_ref[...]
    if not route:
        yf_ref, yb_ref = rest
        yf_ref[...] = y
        yb_ref[...] = y.astype(BF16)
        return
    wt_ref, rb_ref, yf_ref, yb_ref, gate_ref = rest
    y_hi = y.astype(BF16)
    y_lo = (y - y_hi.astype(F32)).astype(BF16)
    yf_ref[...] = y
    yb_ref[...] = y_hi
    wt = wt_ref[...]
    w_hi = wt.astype(BF16)
    w_lo = (wt - w_hi.astype(F32)).astype(BF16)
    w_both = jnp.concatenate([w_hi, w_lo], axis=0)
    nt = (((1,), (1,)), ((), ()))
    part = (lax.dot_general(w_both, y_hi, nt, preferred_element_type=F32)
            + lax.dot_general(w_both, y_lo, nt, preferred_element_type=F32))
    logit = part[:N_EXPERTS] + part[N_EXPERTS:]
    scores = jax.nn.sigmoid(logit)
    biased = scores + rb_ref[...]
    gates, sel = _route_rows(scores, biased)
    gate_ref[...] = jnp.concatenate([gates, sel], axis=0)


def _ln_residual(x, h, g, b, router=None, name="ln"):
    M, D = x.shape
    tm = _pick(M, (256, 128, 64, 32, 16))
    row = pl.BlockSpec((tm, D), lambda i: (i, 0))
    vec = pl.BlockSpec((1, D), lambda i: (0, 0))
    in_specs = [row, row, vec, vec]
    operands = [x, h, g.reshape(1, D), b.reshape(1, D)]
    out_specs = [row, row]
    out_shape = [jax.ShapeDtypeStruct((M, D), F32), jax.ShapeDtypeStruct((M, D), BF16)]
    if router is not None:
        wt, rb = router
        in_specs += [pl.BlockSpec((N_EXPERTS, D), lambda i: (0, 0)),
                     pl.BlockSpec((N_EXPERTS, 1), lambda i: (0, 0))]
        operands += [wt, rb]
        out_specs.append(pl.BlockSpec((2 * N_EXPERTS, tm), lambda i: (0, i)))
        out_shape.append(jax.ShapeDtypeStruct((2 * N_EXPERTS, M), F32))
    return pl.pallas_call(
        functools.partial(_ln_body, route=router is not None),
        grid=(M // tm,),
        in_specs=in_specs,
        out_specs=out_specs,
        out_shape=out_shape,
        compiler_params=_params(("arbitrary",)),
        name=name,
    )(*operands)


def _moe_up_body(te_ref, nt_ref, x_ref, w1_ref, w3_ref, h_ref, w1b_ref, w3b_ref):
    i = pl.program_id(1)
    prev = te_ref[jnp.maximum(i - 1, 0)]
    fresh = jnp.logical_or(i == 0, te_ref[i] != prev)

    @pl.when(jnp.logical_and(fresh, i < nt_ref[0]))
    def _():
        w1b_ref[...] = w1_ref[...].astype(BF16)
        w3b_ref[...] = w3_ref[...].astype(BF16)

    @pl.when(i < nt_ref[0])
    def _():
        xv = x_ref[...]
        a = jnp.dot(xv, w1b_ref[...], preferred_element_type=F32)
        b = jnp.dot(xv, w3b_ref[...], preferred_element_type=F32)
        h_ref[...] = (a * jax.nn.sigmoid(a) * b).astype(BF16)


def _moe_down_body(te_ref, nt_ref, h_ref, w2_ref, g_ref, y_ref, w2b_ref):
    i = pl.program_id(1)
    prev = te_ref[jnp.maximum(i - 1, 0)]
    fresh = jnp.logical_or(i == 0, te_ref[i] != prev)

    @pl.when(jnp.logical_and(fresh, i < nt_ref[0]))
    def _():
        w2b_ref[...] = w2_ref[...].astype(BF16)

    @pl.when(i < nt_ref[0])
    def _():
        y = jnp.dot(h_ref[...], w2b_ref[...], preferred_element_type=F32)
        y_ref[...] = y * g_ref[...]


def _moe_experts(xs, row_gate, tile_expert, n_tiles, w1, w3, w2, layer):
    P, D = xs.shape
    tm = MOE_TM
    n_i = P // tm
    de = w1.shape[-1]
    tn1 = _pick(de, (512, 256, 128))
    tn2 = _pick(D, (2048, 1024, 512, 256, 128))

    def tile_row(i, nt):
        return jnp.minimum(i, nt[0] - 1)

    h = pl.pallas_call(
        _moe_up_body,
        grid_spec=pltpu.PrefetchScalarGridSpec(
            num_scalar_prefetch=2,
            grid=(de // tn1, n_i),
            in_specs=[pl.BlockSpec((tm, D), lambda j, i, te, nt: (tile_row(i, nt), 0)),
                      pl.BlockSpec((None, None, D, tn1), lambda j, i, te, nt: (layer, te[tile_row(i, nt)], 0, j)),
                      pl.BlockSpec((None, None, D, tn1), lambda j, i, te, nt: (layer, te[tile_row(i, nt)], 0, j))],
            out_specs=pl.BlockSpec((tm, tn1), lambda j, i, te, nt: (tile_row(i, nt), j)),
            scratch_shapes=[pltpu.VMEM((D, tn1), BF16), pltpu.VMEM((D, tn1), BF16)]),
        out_shape=jax.ShapeDtypeStruct((P, de), BF16),
        compiler_params=_params(("arbitrary", "arbitrary")),
        name="moe_up",
    )(tile_expert, n_tiles, xs, w1, w3)

    y = pl.pallas_call(
        _moe_down_body,
        grid_spec=pltpu.PrefetchScalarGridSpec(
            num_scalar_prefetch=2,
            grid=(D // tn2, n_i),
            in_specs=[pl.BlockSpec((tm, de), lambda j, i, te, nt: (tile_row(i, nt), 0)),
                      pl.BlockSpec((None, None, de, tn2), lambda j, i, te, nt: (layer, te[tile_row(i, nt)], 0, j)),
                      pl.BlockSpec((tm, 1), lambda j, i, te, nt: (tile_row(i, nt), 0))],
            out_specs=pl.BlockSpec((tm, tn2), lambda j, i, te, nt: (tile_row(i, nt), j)),
            scratch_shapes=[pltpu.VMEM((de, tn2), BF16)]),
        out_shape=jax.ShapeDtypeStruct((P, D), F32),
        compiler_params=_params(("arbitrary", "arbitrary")),
        name="moe_down",
    )(tile_expert, n_tiles, h, w2, row_gate.reshape(P, 1))
    return y


def _moe_layer(x1b, gate_sel, n_tok, w1, w3, w2, layer):
    Mp, D = x1b.shape
    E = N_EXPERTS
    tm = MOE_TM
    gates = gate_sel[:E, :n_tok].T
    sel = gate_sel[E:, :n_tok].T
    e0 = jnp.argmax(sel, axis=1).astype(jnp.int32)
    e1 = (E - 1 - jnp.argmax(sel[:, ::-1], axis=1)).astype(jnp.int32)
    pair_e = jnp.stack([e0, e1], axis=1)
    pair_g = jnp.take_along_axis(gates, pair_e, axis=1)
    flat_e = pair_e.reshape(-1)
    flat_g = pair_g.reshape(-1)
    flat_t = jnp.repeat(jnp.arange(n_tok, dtype=jnp.int32), TOP_K)
    onehot = (flat_e[:, None] == jnp.arange(E, dtype=jnp.int32)[None, :]).astype(jnp.int32)
    csum = jnp.cumsum(onehot, axis=0)
    counts = csum[-1]
    rank = jnp.take_along_axis(csum, flat_e[:, None], axis=1)[:, 0] - 1
    padded = (counts + tm - 1) // tm * tm
    pend = jnp.cumsum(padded)
    pstart = pend - padded
    dest = pstart[flat_e] + rank
    P = _round_up(TOP_K * n_tok, tm) + E * tm
    row_src = jnp.zeros((P,), jnp.int32).at[dest].set(flat_t)
    row_gate = jnp.zeros((P,), F32).at[dest].set(flat_g)
    n_tiles = (pend[-1] // tm).astype(jnp.int32).reshape(1)
    tile_start = jnp.arange(P // tm, dtype=jnp.int32) * tm
    tile_expert = jnp.minimum(jnp.searchsorted(pend, tile_start, side='right'), E - 1).astype(jnp.int32)
    xs = jnp.take(x1b, row_src, axis=0)
    ys = _moe_experts(xs, row_gate, tile_expert, n_tiles, w1, w3, w2, layer)
    pos = dest.reshape(n_tok, TOP_K)
    f = jnp.take(ys, pos[:, 0], axis=0) + jnp.take(ys, pos[:, 1], axis=0)
    return jnp.concatenate([f, jnp.zeros((Mp - n_tok, D), F32)], axis=0)


def _softmax_step(s, mask, m_ref, l_ref, idx):
    s = jnp.where(mask, s, NEG_INF)
    m_prev = m_ref[idx]
    m_new = jnp.maximum(m_prev, jnp.max(s, axis=1, keepdims=True))
    alpha = jnp.exp(m_prev - m_new)
    p = jnp.exp(s - m_new[:, :1])
    l_ref[idx] = alpha * l_ref[idx] + jnp.sum(p, axis=1, keepdims=True)
    m_ref[idx] = m_new
    return p, alpha


def _attn_mask(qi, ki, tq, tk, q_pos0, n_valid_k):
    shift = int(math.log2(CHUNK))
    q_pos = q_pos0 + qi * tq + lax.broadcasted_iota(jnp.int32, (tq, tk), 0)
    k_pos = ki * tk + lax.broadcasted_iota(jnp.int32, (tq, tk), 1)
    vis = lax.shift_right_logical(k_pos, shift) <= lax.shift_right_logical(q_pos, shift)
    return jnp.logical_and(vis, k_pos < n_valid_k)


def _flash_mla_body(qi_ref, ki_ref, last_ref, q_ref, k_ref, v_ref, o_ref, m_ref, l_ref, acc_ref, *,
                    G, tq, tk, q_pos0, n_valid_k):
    step = pl.program_id(2)
    qi = qi_ref[step]
    ki = ki_ref[step]

    @pl.when(ki == 0)
    def _():
        m_ref[...] = jnp.full(m_ref.shape, NEG_INF, F32)
        l_ref[...] = jnp.zeros(l_ref.shape, F32)
        acc_ref[...] = jnp.zeros(acc_ref.shape, F32)

    mask = _attn_mask(qi, ki, tq, tk, q_pos0, n_valid_k)
    nt = (((1,), (1,)), ((), ()))
    for g in range(G):
        q = q_ref[:, g * MLA_HEAD_PAD:(g + 1) * MLA_HEAD_PAD]
        k = k_ref[:, g * MLA_HEAD_PAD:(g + 1) * MLA_HEAD_PAD]
        s = lax.dot_general(q, k, nt, preferred_element_type=F32)
        p, alpha = _softmax_step(s, mask, m_ref, l_ref, g)
        pv = jnp.dot(p.astype(BF16), v_ref[:, g * MLA_V:(g + 1) * MLA_V], preferred_element_type=F32)
        acc_ref[g] = alpha[:, :1] * acc_ref[g] + pv

    @pl.when(last_ref[step] == 1)
    def _():
        for g in range(G):
            o_ref[:, g * MLA_V:(g + 1) * MLA_V] = (acc_ref[g] / l_ref[g][:, :1]).astype(BF16)


def _flash_diff_body(qi_ref, ki_ref, last_ref, q_ref, k_ref, v_ref, lq1_ref, lk1_ref, lq2_ref, lk2_ref,
                     gs_ref, o_ref, m_ref, l_ref, acc_ref, *, G, tq, tk, q_pos0, n_valid_k, lam_init):
    step = pl.program_id(2)
    qi = qi_ref[step]
    ki = ki_ref[step]
    hd = 2 * LANES

    @pl.when(ki == 0)
    def _():
        m_ref[...] = jnp.full(m_ref.shape, NEG_INF, F32)
        l_ref[...] = jnp.zeros(l_ref.shape, F32)
        acc_ref[...] = jnp.zeros(acc_ref.shape, F32)

    mask = _attn_mask(qi, ki, tq, tk, q_pos0, n_valid_k)
    nt = (((1,), (1,)), ((), ()))
    for g in range(G):
        v = v_ref[:, g * hd:(g + 1) * hd]
        for c in range(2):
            col = slice(g * hd + c * LANES, g * hd + (c + 1) * LANES)
            s = lax.dot_general(q_ref[:, col], k_ref[:, col], nt, preferred_element_type=F32)
            p, alpha = _softmax_step(s, mask, m_ref, l_ref, 2 * g + c)
            pv = jnp.dot(p.astype(BF16), v, preferred_element_type=F32)
            acc_ref[2 * g + c] = alpha[:, :1] * acc_ref[2 * g + c] + pv

    @pl.when(last_ref[step] == 1)
    def _():
        lam = (jnp.exp(jnp.sum(lq1_ref[...] * lk1_ref[...], axis=-1, keepdims=True))
               - jnp.exp(jnp.sum(lq2_ref[...] * lk2_ref[...], axis=-1, keepdims=True)) + lam_init)
        for g in range(G):
            o0 = acc_ref[2 * g] / l_ref[2 * g][:, :1]
            o1 = acc_ref[2 * g + 1] / l_ref[2 * g + 1][:, :1]
            o = o0 - lam * o1
            o = o * lax.rsqrt(jnp.mean(o * o, axis=-1, keepdims=True) + RMS_EPS) * gs_ref[...]
            o_ref[:, g * hd:(g + 1) * hd] = (o * (1.0 - lam_init)).astype(BF16)


def _flash(kind, q, k, v, *, n_batch, sq, sk, tq, tk, q_row0, n_heads, q_pos0, n_valid_k, causal,
           extras=(), lam_init=0.0):
    if kind == "mla":
        dqk, dv, n_state = MLA_HEAD_PAD, MLA_V, 1
    else:
        dqk, dv, n_state = 2 * LANES, 2 * LANES, 2
    G = _pick(n_heads, (4, 2, 1))
    nq, nk = sq // tq, sk // tk
    pairs = [(a, b) for a in range(nq) for b in range(nk)
             if (not causal) or (b * tk) // CHUNK <= (q_pos0 + a * tq + tq - 1) // CHUNK]
    qi_tab = jnp.asarray([a for a, _ in pairs], jnp.int32)
    ki_tab = jnp.asarray([b for _, b in pairs], jnp.int32)
    last_tab = jnp.asarray([1 if (n + 1 == len(pairs) or pairs[n + 1][0] != a) else 0
                            for n, (a, _) in enumerate(pairs)], jnp.int32)
    for n, (a, b) in enumerate(pairs):
        assert (b == 0) == (n == 0 or pairs[n - 1][0] != a)
    qb0 = q_row0 // tq
    assert q_row0 % tq == 0
    in_specs = [pl.BlockSpec((tq, G * dqk), lambda b, h, s, qt, kt, lt: (qb0 + b * nq + qt[s], h)),
                pl.BlockSpec((tk, G * dqk), lambda b, h, s, qt, kt, lt: (b * nk + kt[s], h)),
                pl.BlockSpec((tk, G * dv), lambda b, h, s, qt, kt, lt: (b * nk + kt[s], h))]
    operands = [q, k, v]
    common = dict(G=G, tq=tq, tk=tk, q_pos0=q_pos0, n_valid_k=n_valid_k)
    if kind == "mla":
        body = functools.partial(_flash_mla_body, **common)
    else:
        body = functools.partial(_flash_diff_body, lam_init=lam_init, **common)
        for e in extras:
            in_specs.append(pl.BlockSpec(e.shape, lambda b, h, s, qt, kt, lt: (0, 0)))
            operands.append(e)
    return pl.pallas_call(
        body,
        grid_spec=pltpu.PrefetchScalarGridSpec(
            num_scalar_prefetch=3,
            grid=(n_batch, n_heads // G, len(pairs)),
            in_specs=in_specs,
            out_specs=pl.BlockSpec((tq, G * dv), lambda b, h, s, qt, kt, lt: (b * nq + qt[s], h)),
            scratch_shapes=[pltpu.VMEM((G * n_state, tq, LANES), F32),
                            pltpu.VMEM((G * n_state, tq, LANES), F32),
                            pltpu.VMEM((G * n_state, tq, dv), F32)]),
        out_shape=jax.ShapeDtypeStruct((n_batch * sq, n_heads * dv), BF16),
        compiler_params=_params(("arbitrary", "arbitrary", "arbitrary")),
        name="flash_" + kind,
    )(qi_tab, ki_tab, last_tab, *operands)


def _pool_body(halo_ref, x_ref, o_ref, e_ref, *, tm, tiles_per_seq, n_hist, halo):
    i = pl.program_id(0)
    first = (i % tiles_per_seq) == 0
    hv = halo_ref[...]
    if n_hist == 0:
        hv = jnp.where(first, jnp.zeros_like(hv), hv)
    e_ref[0:halo, :] = hv
    e_ref[halo:halo + tm, :] = x_ref[...]
    t_glob = (i % tiles_per_seq) * tm + lax.broadcasted_iota(jnp.int32, (tm, 1), 0)
    avail = (t_glob + 1 + n_hist).astype(F32)
    D = x_ref.shape[1]
    gw = D // len(POOL_WINDOWS)
    for g, win in enumerate(POOL_WINDOWS):
        cols = slice(g * gw, (g + 1) * gw)
        s = e_ref[halo:halo + tm, cols]
        xg = s
        for kk in range(1, win):
            s = s + e_ref[halo - kk:halo - kk + tm, cols]
        cnt = jnp.minimum(avail, jnp.float32(win))
        o_ref[:, cols] = (s / cnt - xg).astype(BF16)


def _pool_premix(x, halo_arr, *, n_tiles, tm, tiles_per_seq, n_hist, halo_index):
    D = x.shape[1]
    halo = max(POOL_WINDOWS)
    return pl.pallas_call(
        functools.partial(_pool_body, tm=tm, tiles_per_seq=tiles_per_seq, n_hist=n_hist, halo=halo),
        grid=(n_tiles,),
        in_specs=[pl.BlockSpec((halo, D), halo_index),
                  pl.BlockSpec((tm, D), lambda i: (i, 0))],
        out_specs=pl.BlockSpec((tm, D), lambda i: (i, 0)),
        out_shape=jax.ShapeDtypeStruct((n_tiles * tm, D), BF16),
        scratch_shapes=[pltpu.VMEM((halo + tm, D), F32)],
        compiler_params=_params(("arbitrary",)),
        name="pool_premix",
    )(halo_arr, x)


def _conv_body(halo_ref, u_ref, w_ref, bdw_ref, g_ref, b_ref, o_ref, e_ref, c_ref, *, tm, tiles_per_seq,
               zero_first, halo):
    i = pl.program_id(0)
    hv = halo_ref[...]
    if zero_first:
        hv = jnp.where((i % tiles_per_seq) == 0, jnp.zeros_like(hv), hv)
    e_ref[0:halo, :] = hv
    e_ref[halo:halo + tm, :] = u_ref[...]
    D = u_ref.shape[1]
    cw = _pick(D, (512, 256, 128))
    off = halo - (CONV_WIDTH - 1)
    for cc in range(D // cw):
        cols = slice(cc * cw, (cc + 1) * cw)
        acc = e_ref[off:off + tm, cols] * w_ref[0:1, cols]
        for k in range(1, CONV_WIDTH):
            acc = acc + e_ref[off + k:off + k + tm, cols] * w_ref[k:k + 1, cols]
        c_ref[:, cols] = acc + bdw_ref[:, cols]
    c = c_ref[...]
    mu = jnp.mean(c, axis=-1, keepdims=True)
    cc0 = c - mu
    var = jnp.mean(cc0 * cc0, axis=-1, keepdims=True)
    y = cc0 * lax.rsqrt(var + LN_EPS) * g_ref[...] + b_ref[...]
    o_ref[...] = (y * jax.nn.sigmoid(y)).astype(BF16)


def _conv_dw(u, halo_arr, w_dw, b_dw, ln_g, ln_b, *, n_tiles, tm, tiles_per_seq, zero_first, halo_index):
    D = u.shape[1]
    halo = _round_up(CONV_WIDTH - 1, 8)
    kw = w_dw.shape[0]
    vec = pl.BlockSpec((1, D), lambda i: (0, 0))
    return pl.pallas_call(
        functools.partial(_conv_body, tm=tm, tiles_per_seq=tiles_per_seq, zero_first=zero_first, halo=halo),
        grid=(n_tiles,),
        in_specs=[pl.BlockSpec((halo, D), halo_index),
                  pl.BlockSpec((tm, D), lambda i: (i, 0)),
                  pl.BlockSpec((kw, D), lambda i: (0, 0)), vec, vec, vec],
        out_specs=pl.BlockSpec((tm, D), lambda i: (i, 0)),
        out_shape=jax.ShapeDtypeStruct((n_tiles * tm, D), BF16),
        scratch_shapes=[pltpu.VMEM((halo + tm, D), F32), pltpu.VMEM((tm, D), F32)],
        compiler_params=_params(("arbitrary",)),
        name="conv_dw",
    )(halo_arr, u, w_dw, b_dw.reshape(1, D), ln_g.reshape(1, D), ln_b.reshape(1, D))


def _rope_tables(pos, half, width):
    inv_freq = ROPE_THETA ** (-jnp.arange(half, dtype=F32) / half)
    ang = pos.astype(F32)[:, None] * inv_freq[None, :]
    cos, sin = jnp.cos(ang), jnp.sin(ang)
    n = pos.shape[0]
    c = jnp.concatenate([cos, cos, jnp.ones((n, width - 2 * half), F32)], axis=1)
    s = jnp.concatenate([-sin, sin, jnp.zeros((n, width - 2 * half), F32)], axis=1)
    return c, s


def _stack_rows(prompt_part, sample_part, total_rows):
    pad = total_rows - prompt_part.shape[0] - sample_part.shape[0]
    return jnp.concatenate([prompt_part, sample_part,
                            jnp.zeros((pad, prompt_part.shape[1]), prompt_part.dtype)], axis=0)


def _mla_mixer(xb, pos_rows, past_c, past_pe, w_dq, g_q, w_uq, w_dkv, g_kv, w_ukv, w_o, dims):
    Tp, Ts, Mp = dims
    D = xb.shape[1]
    H = MLA_HEADS
    r = MLA_KV_RANK
    cm, sm = _rope_tables(pos_rows, MLA_ROPE // 2, LANES)

    (qlat,) = _fused_matmul(xb, w_dq, _ep_rms, tn=MLA_Q_RANK, n_cols=MLA_Q_RANK,
                            aux=[(g_q.reshape(1, -1), 'col')], outs=[(MLA_Q_RANK, BF16)], name="mla_dq")
    w_dkv_p = jnp.concatenate([w_dkv, jnp.zeros((D, LANES - MLA_ROPE), F32)], axis=1)
    c_f, c_b, pe_f, pe_b = _fused_matmul(
        xb, w_dkv_p, _ep_mla_dkv, tn=r + LANES, n_cols=r + LANES,
        aux=[(g_kv.reshape(1, r), 'full'), (cm, 'row'), (sm, 'row')],
        outs=[(r, F32), (r, BF16), (LANES, F32), (LANES, BF16)], name="mla_dkv")

    w_uq_h = w_uq.reshape(MLA_Q_RANK, H, MLA_NOPE + MLA_ROPE)
    w_uq_p = jnp.concatenate([w_uq_h, jnp.zeros((MLA_Q_RANK, H, MLA_HEAD_PAD - MLA_NOPE - MLA_ROPE), F32)],
                             axis=2).reshape(MLA_Q_RANK, H * MLA_HEAD_PAD)
    hpt = _pick(H, (4, 2, 1))
    scale = (MLA_NOPE + MLA_ROPE) ** -0.5
    (q_b,) = _fused_matmul(qlat, w_uq_p, functools.partial(_ep_mla_q, scale=scale, heads_per_tile=hpt),
                           tn=hpt * MLA_HEAD_PAD, n_cols=H * MLA_HEAD_PAD,
                           aux=[(cm, 'row'), (sm, 'row')], outs=[(hpt * MLA_HEAD_PAD, BF16)], name="mla_uq")

    w_ukv_h = w_ukv.reshape(r, H, MLA_NOPE + MLA_V)
    w_uk = w_ukv_h[:, :, :MLA_NOPE].reshape(r, H * MLA_NOPE)
    w_uv = w_ukv_h[:, :, MLA_NOPE:].reshape(r, H * MLA_V)
    hpk = _pick(H, (8, 4, 2, 1))

    def kv_up(c_rows, pe_rows):
        (kf,) = _fused_matmul(c_rows, w_uk, functools.partial(_ep_mla_k, heads_per_tile=hpk),
                              tn=hpk * MLA_NOPE, n_cols=H * MLA_NOPE, aux=[(pe_rows, 'row')],
                              outs=[(hpk * MLA_HEAD_PAD, BF16)], name="mla_uk")
        (vv,) = _fused_matmul(c_rows, w_uv, _ep_plain, tn=_pick(H * MLA_V, (1024, 512, 256, 128)),
                              n_cols=H * MLA_V, outs=[(_pick(H * MLA_V, (1024, 512, 256, 128)), BF16)],
                              name="mla_uv")
        return kf, vv

    kf_p, v_p = kv_up(c_b[:Tp], pe_b[:Tp])
    tq = _pick(SEQ, (SEQ_TQ, 256, 128))
    o_p = _flash("mla", q_b, kf_p, v_p, n_batch=BATCH, sq=SEQ, sk=SEQ, tq=tq, tk=tq, q_row0=0,
                 n_heads=H, q_pos0=0, n_valid_k=SEQ, causal=True)

    lk = _round_up(PAST_LEN + DEC_SEQ, LANES)
    padk = lk - PAST_LEN - DEC_SEQ
    c_s = jnp.concatenate([past_c.astype(BF16), c_b[Tp:Tp + Ts].reshape(DEC_BATCH, DEC_SEQ, r),
                           jnp.zeros((DEC_BATCH, padk, r), BF16)], axis=1).reshape(DEC_BATCH * lk, r)
    past_pe_p = jnp.concatenate([past_pe, jnp.zeros(past_pe.shape[:2] + (LANES - MLA_ROPE,), F32)], axis=2)
    pe_s = jnp.concatenate([past_pe_p.astype(BF16), pe_b[Tp:Tp + Ts].reshape(DEC_BATCH, DEC_SEQ, LANES),
                            jnp.zeros((DEC_BATCH, padk, LANES), BF16)], axis=1).reshape(DEC_BATCH * lk, LANES)
    kf_s, v_s = kv_up(c_s, pe_s)
    o_s = _flash("mla", q_b, kf_s, v_s, n_batch=DEC_BATCH, sq=DEC_SEQ, sk=lk, tq=DEC_SEQ, tk=lk, q_row0=Tp,
                 n_heads=H, q_pos0=PAST_LEN, n_valid_k=PAST_LEN + DEC_SEQ, causal=True)

    o_all = _stack_rows(o_p, o_s, Mp)
    tn_o = _pick(D, (512, 256, 128))
    (h,) = _fused_matmul(o_all, w_o, _ep_plain, tn=tn_o, n_cols=D, outs=[(tn_o, F32)], name="mla_wo")
    new_p = (c_f[:Tp].reshape(BATCH, SEQ, r), pe_f[:Tp, :MLA_ROPE].reshape(BATCH, SEQ, MLA_ROPE))
    new_s = (c_f[Tp:Tp + Ts].reshape(DEC_BATCH, DEC_SEQ, r),
             pe_f[Tp:Tp + Ts, :MLA_ROPE].reshape(DEC_BATCH, DEC_SEQ, MLA_ROPE))
    return h, new_p, new_s


def _pool_mixer(x, hist, w_pool, scale, dims):
    Tp, Ts, Mp = dims
    D = x.shape[1]
    halo = max(POOL_WINDOWS)
    n_hist = halo - 1
    tm = _pick(SEQ, (POOL_TM, 128, 64, 32, 16))
    tps = SEQ // tm
    bpt = tm // halo
    d_p = _pool_premix(x, x, n_tiles=Tp // tm, tm=tm, tiles_per_seq=tps, n_hist=0,
                       halo_index=lambda i: (jnp.maximum(i * bpt - 1, 0), 0))
    xs = x[Tp:Tp + Ts].reshape(DEC_BATCH, DEC_SEQ, D)
    halo_s = jnp.concatenate([jnp.zeros((DEC_BATCH, halo - n_hist, D), F32), hist], axis=1
                             ).reshape(DEC_BATCH * halo, D)
    d_s = _pool_premix(xs.reshape(Ts, D), halo_s, n_tiles=DEC_BATCH, tm=DEC_SEQ, tiles_per_seq=1,
                       n_hist=n_hist, halo_index=lambda i: (i, 0))
    d_all = _stack_rows(d_p, d_s, Mp)
    gw = D // len(POOL_WINDOWS)
    (h,) = _fused_matmul(d_all, w_pool, _ep_scale_cols, tn=gw, n_cols=D, x_kblock=True, w_lead_from_j=True,
                         aux=[(scale.reshape(1, D), 'col')], outs=[(gw, F32)], name="pool_mm")
    new_p = x[:Tp].reshape(BATCH, SEQ, D)[:, SEQ - n_hist:]
    new_s = jnp.concatenate([hist, xs], axis=1)[:, -n_hist:]
    return h, new_p, new_s


def _conv_mixer(x, xb, hist, w_pw1, b_pw1, w_dw, b_dw, ln_g, ln_b, w_pw2, b_pw2, dims):
    Tp, Ts, Mp = dims
    D = x.shape[1]
    n_hist = CONV_WIDTH - 1
    halo = _round_up(n_hist, 8)
    u = _glu_matmul(xb, w_pw1, b_pw1.reshape(1, -1))
    tm = _pick(SEQ, (CONV_TM, 32, 16))
    assert tm % halo == 0 or halo % tm == 0
    tps = SEQ // tm
    bpt = tm // halo
    a_p = _conv_dw(u, u, w_dw, b_dw, ln_g, ln_b, n_tiles=Tp // tm, tm=tm, tiles_per_seq=tps, zero_first=True,
                   halo_index=lambda i: (jnp.maximum(i * bpt - 1, 0), 0))
    us = u[Tp:Tp + Ts].reshape(DEC_BATCH, DEC_SEQ, D)
    halo_s = jnp.concatenate([jnp.zeros((DEC_BATCH, halo - n_hist, D), F32), hist], axis=1
                             ).reshape(DEC_BATCH * halo, D)
    a_s = _conv_dw(us.reshape(Ts, D), halo_s, w_dw, b_dw, ln_g, ln_b, n_tiles=DEC_BATCH, tm=DEC_SEQ,
                   tiles_per_seq=1, zero_first=False, halo_index=lambda i: (i, 0))
    a_all = _stack_rows(a_p, a_s, Mp)
    tn = _pick(D, (512, 256, 128))
    (h,) = _fused_matmul(a_all, w_pw2, _ep_bias, tn=tn, n_cols=D, aux=[(b_pw2.reshape(1, D), 'col')],
                         outs=[(tn, F32)], name="conv_pw2")
    new_p = u[:Tp].reshape(BATCH, SEQ, D)[:, SEQ - n_hist:]
    new_s = jnp.concatenate([hist, us], axis=1)[:, -n_hist:]
    return h, new_p, new_s


def _diff_mixer(xb, pos_rows, past_k, past_v, w_qkv, lq1, lk1, lq2, lk2, g_sub, w_o, lam_init, dims):
    Tp, Ts, Mp = dims
    D = xb.shape[1]
    H = DIFF_HEADS
    hd = D // H
    assert hd == 2 * LANES
    cd, sd = _rope_tables(pos_rows, LANES // 2, LANES)
    scale = (hd // 2) ** -0.5
    tn = _pick(D, (512, 256, 128))
    ns = tn // LANES
    tabs = [(cd, 'row'), (sd, 'row')]
    (q_b,) = _fused_matmul(xb, w_qkv, functools.partial(_ep_rope_heads, scale=scale, n_slabs=ns, want_f32=False),
                           tn=tn, n_cols=D, col0=0, aux=tabs, outs=[(tn, BF16)], name="diff_q")
    k_f, k_b = _fused_matmul(xb, w_qkv, functools.partial(_ep_rope_heads, scale=1.0, n_slabs=ns, want_f32=True),
                             tn=tn, n_cols=D, col0=D, aux=tabs, outs=[(tn, F32), (tn, BF16)], name="diff_k")
    v_f, v_b = _fused_matmul(xb, w_qkv, _ep_plain, tn=tn, n_cols=D, col0=2 * D,
                             outs=[(tn, F32), (tn, BF16)], name="diff_v")
    extras = [lq1.reshape(1, -1), lk1.reshape(1, -1), lq2.reshape(1, -1), lk2.reshape(1, -1),
              g_sub.reshape(1, -1)]
    tq = _pick(SEQ, (SEQ_TQ, 256, 128))
    o_p = _flash("diff", q_b, k_b, v_b, n_batch=BATCH, sq=SEQ, sk=SEQ, tq=tq, tk=tq, q_row0=0, n_heads=H,
                 q_pos0=0, n_valid_k=SEQ, causal=True, extras=extras, lam_init=lam_init)
    lk = _round_up(PAST_LEN + DEC_SEQ, LANES)
    padk = lk - PAST_LEN - DEC_SEQ

    def with_cache(past, new_b):
        return jnp.concatenate([past.reshape(DEC_BATCH, PAST_LEN, D).astype(BF16),
                                new_b[Tp:Tp + Ts].reshape(DEC_BATCH, DEC_SEQ, D),
                                jnp.zeros((DEC_BATCH, padk, D), BF16)], axis=1).reshape(DEC_BATCH * lk, D)

    o_s = _flash("diff", q_b, with_cache(past_k, k_b), with_cache(past_v, v_b), n_batch=DEC_BATCH, sq=DEC_SEQ,
                 sk=lk, tq=DEC_SEQ, tk=lk, q_row0=Tp, n_heads=H, q_pos0=PAST_LEN,
                 n_valid_k=PAST_LEN + DEC_SEQ, causal=True, extras=extras, lam_init=lam_init)
    o_all = _stack_rows(o_p, o_s, Mp)
    (h,) = _fused_matmul(o_all, w_o, _ep_plain, tn=tn, n_cols=D, outs=[(tn, F32)], name="diff_wo")
    new_p = (k_f[:Tp].reshape(BATCH, SEQ, H, 2, hd // 2), v_f[:Tp].reshape(BATCH, SEQ, H, hd))
    new_s = (k_f[Tp:Tp + Ts].reshape(DEC_BATCH, DEC_SEQ, H, 2, hd // 2),
             v_f[Tp:Tp + Ts].reshape(DEC_BATCH, DEC_SEQ, H, hd))
    return h, new_p, new_s


def kernel(x_prompt, x_sample, cache_mla_ckv, cache_mla_kpe, state_pool, state_conv, cache_diff_k, cache_diff_v, p_prompt, p_sample, mla_w_dq, mla_g_q, mla_w_uq, mla_w_dkv, mla_g_kv, mla_w_ukv, mla_w_o, pool_w, pool_scale, conv_w_pw1, conv_b_pw1, conv_w_dw, conv_b_dw, conv_ln_g, conv_ln_b, conv_w_pw2, conv_b_pw2, diff_w_qkv, diff_lam_q1, diff_lam_k1, diff_lam_q2, diff_lam_k2, diff_g_sub, diff_w_o, router_w, router_b, moe_w1, moe_w3, moe_w2, ln1_g, ln1_b, ln2_g, ln2_b, ple_w, ple_gate):
    D = D_MODEL
    Tp, Ts = BATCH * SEQ, DEC_BATCH * DEC_SEQ
    T = Tp + Ts
    Mp = _round_up(T, ROW_PAD)
    dims = (Tp, Ts, Mp)

    x = _stack_rows(x_prompt.reshape(Tp, D), x_sample.reshape(Ts, D), Mp)
    xb = x.astype(BF16)
    p_all = jnp.concatenate([p_prompt.reshape(DEPTH, Tp, PLE_DIM), p_sample.reshape(DEPTH, Ts, PLE_DIM),
                             jnp.zeros((DEPTH, Mp - T, PLE_DIM), F32)], axis=1)
    pos_rows = jnp.concatenate([jnp.tile(jnp.arange(SEQ, dtype=jnp.int32), BATCH),
                                jnp.tile(PAST_LEN + jnp.arange(DEC_SEQ, dtype=jnp.int32), DEC_BATCH),
                                jnp.zeros((Mp - T,), jnp.int32)])

    router = (router_w.T, router_b.reshape(N_EXPERTS, 1))

    new_p = {k: [] for k in ('mla_ckv', 'mla_kpe', 'pool', 'conv', 'diff_k', 'diff_v')}
    new_s = {k: [] for k in new_p}
    for i in range(DEPTH):
        kind, j = i % N_MIXERS, i // N_MIXERS
        if kind == 0:
            h, (cp, pp), (cs, ps) = _mla_mixer(
                xb, pos_rows, cache_mla_ckv[j], cache_mla_kpe[j], mla_w_dq[j], mla_g_q[j], mla_w_uq[j],
                mla_w_dkv[j], mla_g_kv[j], mla_w_ukv[j], mla_w_o[j], dims)
            new_p['mla_ckv'].append(cp); new_p['mla_kpe'].append(pp)
            new_s['mla_ckv'].append(cs); new_s['mla_kpe'].append(ps)
        elif kind == 1:
            h, sp, ss = _pool_mixer(x, state_pool[j], pool_w[j], pool_scale[j], dims)
            new_p['pool'].append(sp); new_s['pool'].append(ss)
        elif kind == 2:
            h, sp, ss = _conv_mixer(x, xb, state_conv[j], conv_w_pw1[j], conv_b_pw1[j], conv_w_dw[j],
                                    conv_b_dw[j], conv_ln_g[j], conv_ln_b[j], conv_w_pw2[j], conv_b_pw2[j], dims)
            new_p['conv'].append(sp); new_s['conv'].append(ss)
        else:
            lam_init = 0.8 - 0.6 * math.exp(-0.3 * i)
            h, (kp, vp), (ks, vs) = _diff_mixer(
                xb, pos_rows, cache_diff_k[j], cache_diff_v[j], diff_w_qkv[j], diff_lam_q1[j], diff_lam_k1[j],
                diff_lam_q2[j], diff_lam_k2[j], diff_g_sub[j], diff_w_o[j], lam_init, dims)
            new_p['diff_k'].append(kp); new_p['diff_v'].append(vp)
            new_s['diff_k'].append(ks); new_s['diff_v'].append(vs)

        x1, x1b, gate_sel = _ln_residual(x, h, ln1_g[i], ln1_b[i], router=router, name="ln1_route")
        f = _moe_layer(x1b, gate_sel, T, moe_w1, moe_w3, moe_w2, i)
        x2, x2b = _ln_residual(x1, f, ln2_g[i], ln2_b[i], name="ln2")
        tn = _pick(D, (512, 256, 128))
        x, xb = _fused_matmul(x2b, ple_gate, _ep_ple, tn=tn, n_cols=D, w_lead=(i,),
                              aux=[(x2, 'tile'), (p_all[i], 'row'), (ple_w, 'colw', (i,))],
                              outs=[(tn, F32), (tn, BF16)], name="ple")

    y_prompt = x[:Tp].reshape(BATCH, SEQ, D)
    y_sample = x[Tp:T].reshape(DEC_BATCH, DEC_SEQ, D)
    order = ('mla_ckv', 'mla_kpe', 'pool', 'conv', 'diff_k', 'diff_v')
    return (y_prompt, y_sample) + tuple(jnp.stack(new_p[k]) for k in order) \
        + tuple(jnp.stack(new_s[k]) for k in order)
```

```python
import functools
import math

import jax
import jax.numpy as jnp
from jax import lax
from jax.experimental import pallas as pl
from jax.experimental.pallas import tpu as pltpu

D_MODEL = 4096
BATCH = 8
SEQ = 2048
DEPTH = 4
DEC_BATCH = 8
DEC_SEQ = 16
PAST_LEN = 1024

CHUNK = 64
N_MIXERS = 4
ROPE_THETA = 10000.0
PLE_DIM = 256
LN_EPS = 1e-5
RMS_EPS = 1e-6
NEG_INF = -1e30
DN_ALPHA = (2 * DEPTH) ** 0.25
LOG2E = math.log2(math.e)

MLA_HEADS = 32
MLA_Q_RANK = 1024
MLA_KV_RANK = 512
MLA_NOPE = 128
MLA_ROPE = 64
MLA_V = 128

POOL_WINDOWS = (2, 4, 8, 16)
CONV_WIDTH = 31

DIFF_HEADS = 16

N_EXPERTS = 16
N_GROUPS = 4
TOP_K = 2
D_EXPERT = 1024

LANES = 128
MLA_HEAD_PAD = 2 * LANES
VMEM_LIMIT = 56 * 1024 * 1024
VMEM_BUDGET = 44 * 1024 * 1024
ROW_PAD = 256
MOE_TM = 256
SEQ_TQ = 512
CONV_TM = 64
POOL_TM = 256

BF16 = jnp.bfloat16
F32 = jnp.float32


def _round_up(n, m):
    return (n + m - 1) // m * m


def _pick(n, cands):
    for c in cands:
        if c <= n and n % c == 0:
            return c
    return n


def _params(sem):
    return pltpu.CompilerParams(dimension_semantics=sem, vmem_limit_bytes=VMEM_LIMIT)


def _fused_matmul(x, w, epilogue, *, tn, n_cols, col0=0, w_lead=(), aux=(), outs=(), x_kblock=None,
                  w_lead_from_j=False, row0=0, n_rows=None, name="mm"):
    M = x.shape[0] - row0 if n_rows is None else n_rows
    K = w.shape[-2]
    n_j = n_cols // tn
    assert n_cols % tn == 0 and col0 % tn == 0
    cb = col0 // tn
    nl = len(w.shape) - 2

    def vmem_bytes(tm):
        b = 2 * tm * K * 2 + 2 * K * tn * 4 + K * tn * 2 + tm * tn * 4
        for a in aux:
            if a[1] == 'row':
                b += 2 * tm * a[0].shape[1] * a[0].dtype.itemsize
            elif a[1] == 'tile':
                b += 2 * tm * tn * a[0].dtype.itemsize
            elif a[1] == 'colw':
                b += 2 * a[0].shape[-2] * tn * a[0].dtype.itemsize
        for width, dt in outs:
            b += 2 * tm * width * jnp.dtype(dt).itemsize
        return b

    tm = M
    for c in (1280, 1024, 768, 640, 512, 384, 256, 128, 64, 32, 16):
        if c <= M and M % c == 0 and row0 % c == 0 and vmem_bytes(c) <= VMEM_BUDGET:
            tm = c
            break
    assert M % tm == 0 and row0 % tm == 0
    n_i = M // tm
    rb = row0 // tm

    if x_kblock is None:
        x_spec = pl.BlockSpec((tm, K), lambda j, i: (rb + i, 0))
    else:
        x_spec = pl.BlockSpec((tm, K), lambda j, i: (rb + i, j))
    if w_lead_from_j:
        w_spec = pl.BlockSpec((None,) * nl + (K, tn), lambda j, i: (j,) * nl + (0, 0))
    else:
        w_spec = pl.BlockSpec((None,) * nl + (K, tn), lambda j, i: tuple(w_lead) + (0, cb + j))
    in_specs = [x_spec, w_spec]
    operands = [x, w]
    for a in aux:
        arr, kind = a[0], a[1]
        if kind == 'row':
            in_specs.append(pl.BlockSpec((tm, arr.shape[1]), lambda j, i: (rb + i, 0)))
        elif kind == 'tile':
            in_specs.append(pl.BlockSpec((tm, tn), lambda j, i: (rb + i, j)))
        elif kind == 'col':
            in_specs.append(pl.BlockSpec((1, tn), lambda j, i: (0, cb + j)))
        elif kind == 'full':
            in_specs.append(pl.BlockSpec(arr.shape, lambda j, i: (0, 0)))
        elif kind == 'colw':
            lead2 = tuple(a[2])
            ka = arr.shape[-2]
            in_specs.append(pl.BlockSpec((None,) * len(lead2) + (ka, tn),
                                         lambda j, i, lead2=lead2: lead2 + (0, cb + j)))
        else:
            raise ValueError(kind)
        operands.append(arr)
    out_specs = [pl.BlockSpec((tm, width), lambda j, i: (i, j)) for width, _ in outs]
    out_shape = [jax.ShapeDtypeStruct((M, width * n_j), dt) for width, dt in outs]
    na, no = len(aux), len(outs)

    def body(x_ref, w_ref, *rest):
        aux_refs = rest[:na]
        out_refs = rest[na:na + no]
        wb_ref = rest[na + no]

        @pl.when(pl.program_id(1) == 0)
        def _():
            wb_ref[...] = w_ref[...].astype(BF16)

        acc = jnp.dot(x_ref[...], wb_ref[...], preferred_element_type=F32)
        epilogue(acc, aux_refs, out_refs)

    res = pl.pallas_call(
        body,
        grid=(n_j, n_i),
        in_specs=in_specs,
        out_specs=out_specs,
        out_shape=out_shape,
        scratch_shapes=[pltpu.VMEM((K, tn), BF16)],
        compiler_params=_params(("arbitrary", "arbitrary")),
        name=name,
    )(*operands)
    return res


def _rope_slab(s, c, sn, half):
    if 2 * half == LANES:
        partner = pltpu.roll(s, half, axis=1)
    else:
        lane = lax.broadcasted_iota(jnp.int32, s.shape, 1)
        partner = jnp.where(lane < half, pltpu.roll(s, LANES - half, axis=1), pltpu.roll(s, half, axis=1))
    return s * c + partner * sn


def _ep_plain(acc, aux_refs, out_refs):
    for o in out_refs:
        o[...] = acc.astype(o.dtype)


def _ep_bias(acc, aux_refs, out_refs):
    out_refs[0][...] = (acc + aux_refs[0][...]).astype(out_refs[0].dtype)


def _ep_scale_cols(acc, aux_refs, out_refs):
    out_refs[0][...] = (acc * aux_refs[0][...]).astype(out_refs[0].dtype)


def _ep_rms(acc, aux_refs, out_refs):
    g = aux_refs[0][...]
    y = acc * lax.rsqrt(jnp.mean(acc * acc, axis=-1, keepdims=True) + RMS_EPS) * g
    out_refs[0][...] = y.astype(out_refs[0].dtype)


def _ep_mla_dkv(acc, aux_refs, out_refs):
    g_ref, c_ref, s_ref = aux_refs
    cf_ref, cb_ref, pf_ref, pb_ref = out_refs
    r = MLA_KV_RANK
    lat = acc[:, :r]
    cn = lat * lax.rsqrt(jnp.mean(lat * lat, axis=-1, keepdims=True) + RMS_EPS) * g_ref[...]
    cf_ref[...] = cn
    cb_ref[...] = cn.astype(BF16)
    pe = _rope_slab(acc[:, r:r + LANES], c_ref[...], s_ref[...], MLA_ROPE // 2)
    pf_ref[...] = pe
    pb_ref[...] = pe.astype(BF16)


def _ep_mla_q(acc, aux_refs, out_refs, *, scale, heads_per_tile):
    c = aux_refs[0][...]
    sn = aux_refs[1][...]
    o = out_refs[0]
    for h in range(heads_per_tile):
        a = h * MLA_HEAD_PAD
        o[:, a:a + LANES] = (acc[:, a:a + LANES] * scale).astype(BF16)
        pe = _rope_slab(acc[:, a + LANES:a + 2 * LANES], c, sn, MLA_ROPE // 2)
        o[:, a + LANES:a + 2 * LANES] = (pe * scale).astype(BF16)


def _ep_mla_k(acc, aux_refs, out_refs, *, heads_per_tile):
    pe = aux_refs[0][...]
    o = out_refs[0]
    for h in range(heads_per_tile):
        o[:, h * MLA_HEAD_PAD:h * MLA_HEAD_PAD + LANES] = acc[:, h * LANES:(h + 1) * LANES].astype(BF16)
        o[:, h * MLA_HEAD_PAD + LANES:(h + 1) * MLA_HEAD_PAD] = pe


def _ep_rope_heads(acc, aux_refs, out_refs, *, scale, n_slabs, want_f32):
    c = aux_refs[0][...]
    sn = aux_refs[1][...]
    for h in range(n_slabs):
        sl = slice(h * LANES, (h + 1) * LANES)
        y = _rope_slab(acc[:, sl], c, sn, LANES // 2)
        if want_f32:
            out_refs[0][:, sl] = y
            out_refs[1][:, sl] = y.astype(BF16)
        else:
            out_refs[0][:, sl] = (y * scale).astype(BF16)


def _ep_ple(acc, aux_refs, out_refs):
    x2_ref, p_ref, pw_ref = aux_refs
    p = p_ref[...]
    w = pw_ref[...]
    p_hi = p.astype(BF16)
    p_lo = (p - p_hi.astype(F32)).astype(BF16)
    w_hi = w.astype(BF16)
    w_lo = (w - w_hi.astype(F32)).astype(BF16)
    pw = (jnp.dot(p_hi, w_hi, preferred_element_type=F32) + jnp.dot(p_lo, w_hi, preferred_element_type=F32)
          + jnp.dot(p_hi, w_lo, preferred_element_type=F32))
    y = x2_ref[...] + pw * jax.nn.sigmoid(acc)
    out_refs[0][...] = y
    if len(out_refs) > 1:
        out_refs[1][...] = y.astype(BF16)


def _glu_matmul(x, w, b, *, name="glu"):
    M, K = x.shape
    D = w.shape[1] // 2
    tn = _pick(D, (512, 256, 128))
    n_j = D // tn
    tm = _pick(M, (1280, 1024, 640, 512, 256, 128, 64, 32, 16))
    while 2 * tm * K * 2 + 4 * K * tn * 4 + 2 * K * tn * 2 + 4 * tm * tn * 4 > VMEM_BUDGET and tm % 2 == 0 \
            and M % (tm // 2) == 0:
        tm //= 2
    n_i = M // tm

    def body(x_ref, wa_ref, wg_ref, ba_ref, bg_ref, o_ref, wab_ref, wgb_ref):
        @pl.when(pl.program_id(1) == 0)
        def _():
            wab_ref[...] = wa_ref[...].astype(BF16)
            wgb_ref[...] = wg_ref[...].astype(BF16)

        xv = x_ref[...]
        a = jnp.dot(xv, wab_ref[...], preferred_element_type=F32) + ba_ref[...]
        g = jnp.dot(xv, wgb_ref[...], preferred_element_type=F32) + bg_ref[...]
        o_ref[...] = a * jax.nn.sigmoid(g)

    return pl.pallas_call(
        body,
        grid=(n_j, n_i),
        in_specs=[pl.BlockSpec((tm, K), lambda j, i: (i, 0)),
                  pl.BlockSpec((K, tn), lambda j, i: (0, j)),
                  pl.BlockSpec((K, tn), lambda j, i: (0, n_j + j)),
                  pl.BlockSpec((1, tn), lambda j, i: (0, j)),
                  pl.BlockSpec((1, tn), lambda j, i: (0, n_j + j))],
        out_specs=pl.BlockSpec((tm, tn), lambda j, i: (i, j)),
        out_shape=jax.ShapeDtypeStruct((M, D), F32),
        scratch_shapes=[pltpu.VMEM((K, tn), BF16), pltpu.VMEM((K, tn), BF16)],
        compiler_params=_params(("arbitrary", "arbitrary")),
        name=name,
    )(x, w, w, b, b)


def _route_rows(scores, biased):
    epg = N_EXPERTS // N_GROUPS
    one = jnp.float32(1.0)
    zero = jnp.float32(0.0)
    rows = [biased[e:e + 1, :] for e in range(N_EXPERTS)]
    sel_in = []
    gscore = []
    for g in range(N_GROUPS):
        v = rows[g * epg:(g + 1) * epg]
        gs = None
        for k in range(epg):
            rank = None
            for m in range(epg):
                if m == k:
                    continue
                ahead = (v[m] > v[k]) if m > k else (v[m] >= v[k])
                t = jnp.where(ahead, one, zero)
                rank = t if rank is None else rank + t
            s = jnp.where(rank < TOP_K, one, zero)
            sel_in.append(s)
            contrib = s * v[k]
            gs = contrib if gs is None else gs + contrib
        gscore.append(gs)
    gsel = []
    for g in range(N_GROUPS):
        rank = None
        for m in range(N_GROUPS):
            if m == g:
                continue
            ahead = (gscore[m] > gscore[g]) if m > g else (gscore[m] >= gscore[g])
            t = jnp.where(ahead, one, zero)
            rank = t if rank is None else rank + t
        gsel.append(jnp.where(rank < 1, one, zero))
    sel = [sel_in[e] * gsel[e // epg] for e in range(N_EXPERTS)]
    denom = None
    for e in range(N_EXPERTS):
        t = sel[e] * scores[e:e + 1, :]
        denom = t if denom is None else denom + t
    gates = [sel[e] * scores[e:e + 1, :] / denom for e in range(N_EXPERTS)]
    return jnp.concatenate(gates, axis=0), jnp.concatenate(sel, axis=0)


def _ln_body(x_ref, h_ref, g_ref, b_ref, *rest, route):
    x = DN_ALPHA * x_ref[...] + h_ref[...]
    mu = jnp.mean(x, axis=-1, keepdims=True)
    xc = x - mu
    var = jnp.mean(xc * xc, axis=-1, keepdims=True)
    y = xc * lax.rsqrt(var + LN_EPS) * g_ref[...] + b_ref[...]
    if not route:
        yf_ref, yb_ref = rest
        yf_ref[...] = y
        yb_ref[...] = y.astype(BF16)
        return
    wt_ref, rb_ref, yf_ref, yb_ref, gate_ref = rest
    y_hi = y.astype(BF16)
    y_lo = (y - y_hi.astype(F32)).astype(BF16)
    yf_ref[...] = y
    yb_ref[...] = y_hi
    wt = wt_ref[...]
    w_hi = wt.astype(BF16)
    w_lo = (wt - w_hi.astype(F32)).astype(BF16)
    w_both = jnp.concatenate([w_hi, w_lo], axis=0)
    nt = (((1,), (1,)), ((), ()))
    part = (lax.dot_general(w_both, y_hi, nt, preferred_element_type=F32)
            + lax.dot_general(w_both, y_lo, nt, preferred_element_type=F32))
    logit = part[:N_EXPERTS] + part[N_EXPERTS:]
    scores = jax.nn.sigmoid(logit)
    biased = scores + rb_ref[...]
    gates, sel = _route_rows(scores, biased)
    gate_ref[...] = jnp.concatenate([gates, sel], axis=0)


def _ln_residual(x, h, g, b, router=None, name="ln"):
    M, D = x.shape
    tm = _pick(M, (256, 128, 64, 32, 16))
    row = pl.BlockSpec((tm, D), lambda i: (i, 0))
    vec = pl.BlockSpec((1, D), lambda i: (0, 0))
    in_specs = [row, row, vec, vec]
    operands = [x, h, g.reshape(1, D), b.reshape(1, D)]
    out_specs = [row, row]
    out_shape = [jax.ShapeDtypeStruct((M, D), F32), jax.ShapeDtypeStruct((M, D), BF16)]
    if router is not None:
        wt, rb = router
        in_specs += [pl.BlockSpec((N_EXPERTS, D), lambda i: (0, 0)),
                     pl.BlockSpec((N_EXPERTS, 1), lambda i: (0, 0))]
        operands += [wt, rb]
        out_specs.append(pl.BlockSpec((2 * N_EXPERTS, tm), lambda i: (0, i)))
        out_shape.append(jax.ShapeDtypeStruct((2 * N_EXPERTS, M), F32))
    return pl.pallas_call(
        functools.partial(_ln_body, route=router is not None),
        grid=(M // tm,),
        in_specs=in_specs,
        out_specs=out_specs,
        out_shape=out_shape,
        compiler_params=_params(("arbitrary",)),
        name=name,
    )(*operands)


def _moe_up_body(te_ref, nt_ref, x_ref, w1_ref, w3_ref, h_ref, w1b_ref, w3b_ref):
    i = pl.program_id(1)
    prev = te_ref[jnp.maximum(i - 1, 0)]
    fresh = jnp.logical_or(i == 0, te_ref[i] != prev)

    @pl.when(jnp.logical_and(fresh, i < nt_ref[0]))
    def _():
        w1b_ref[...] = w1_ref[...].astype(BF16)
        w3b_ref[...] = w3_ref[...].astype(BF16)

    @pl.when(i < nt_ref[0])
    def _():
        xv = x_ref[...]
        a = jnp.dot(xv, w1b_ref[...], preferred_element_type=F32)
        b = jnp.dot(xv, w3b_ref[...], preferred_element_type=F32)
        h_ref[...] = (a * jax.nn.sigmoid(a) * b).astype(BF16)

    @pl.when(i >= nt_ref[0])
    def _():
        h_ref[...] = jnp.zeros(h_ref.shape, BF16)


def _moe_down_body(te_ref, nt_ref, h_ref, w2_ref, y_ref, w2b_ref):
    i = pl.program_id(1)
    prev = te_ref[jnp.maximum(i - 1, 0)]
    fresh = jnp.logical_or(i == 0, te_ref[i] != prev)

    @pl.when(jnp.logical_and(fresh, i < nt_ref[0]))
    def _():
        w2b_ref[...] = w2_ref[...].astype(BF16)

    @pl.when(i < nt_ref[0])
    def _():
        y_ref[...] = jnp.dot(h_ref[...], w2b_ref[...], preferred_element_type=F32)

    @pl.when(i >= nt_ref[0])
    def _():
        y_ref[...] = jnp.zeros(y_ref.shape, F32)


def _moe_experts(xs, tile_expert, n_tiles, w1, w3, w2, layer):
    P, D = xs.shape
    tm = MOE_TM
    n_i = P // tm
    de = w1.shape[-1]
    tn1 = _pick(de, (512, 256, 128))
    tn2 = _pick(D, (2048, 1024, 512, 256, 128))

    def tile_row(i, nt):
        return jnp.minimum(i, nt[0] - 1)

    h = pl.pallas_call(
        _moe_up_body,
        grid_spec=pltpu.PrefetchScalarGridSpec(
            num_scalar_prefetch=2,
            grid=(de // tn1, n_i),
            in_specs=[pl.BlockSpec((tm, D), lambda j, i, te, nt: (tile_row(i, nt), 0)),
                      pl.BlockSpec((None, None, D, tn1), lambda j, i, te, nt: (layer, te[tile_row(i, nt)], 0, j)),
                      pl.BlockSpec((None, None, D, tn1), lambda j, i, te, nt: (layer, te[tile_row(i, nt)], 0, j))],
            out_specs=pl.BlockSpec((tm, tn1), lambda j, i, te, nt: (i, j)),
            scratch_shapes=[pltpu.VMEM((D, tn1), BF16), pltpu.VMEM((D, tn1), BF16)]),
        out_shape=jax.ShapeDtypeStruct((P, de), BF16),
        compiler_params=_params(("arbitrary", "arbitrary")),
        name="moe_up",
    )(tile_expert, n_tiles, xs, w1, w3)

    y = pl.pallas_call(
        _moe_down_body,
        grid_spec=pltpu.PrefetchScalarGridSpec(
            num_scalar_prefetch=2,
            grid=(D // tn2, n_i),
            in_specs=[pl.BlockSpec((tm, de), lambda j, i, te, nt: (tile_row(i, nt), 0)),
                      pl.BlockSpec((None, None, de, tn2), lambda j, i, te, nt: (layer, te[tile_row(i, nt)], 0, j))],
            out_specs=pl.BlockSpec((tm, tn2), lambda j, i, te, nt: (i, j)),
            scratch_shapes=[pltpu.VMEM((de, tn2), BF16)]),
        out_shape=jax.ShapeDtypeStruct((P, D), F32),
        compiler_params=_params(("arbitrary", "arbitrary")),
        name="moe_down",
    )(tile_expert, n_tiles, h, w2)
    return y


def _moe_layer(x1b, gate_sel, n_tok, w1, w3, w2, layer):
    Mp, D = x1b.shape
    E = N_EXPERTS
    tm = MOE_TM
    gates = gate_sel[:E, :n_tok].T
    sel = gate_sel[E:, :n_tok].T
    e0 = jnp.argmax(sel, axis=1).astype(jnp.int32)
    e1 = (E - 1 - jnp.argmax(sel[:, ::-1], axis=1)).astype(jnp.int32)
    pair_e = jnp.stack([e0, e1], axis=1)
    pair_g = jnp.take_along_axis(gates, pair_e, axis=1)
    flat_e = pair_e.reshape(-1)
    flat_t = jnp.repeat(jnp.arange(n_tok, dtype=jnp.int32), TOP_K)
    onehot = (flat_e[:, None] == jnp.arange(E, dtype=jnp.int32)[None, :]).astype(jnp.int32)
    csum = jnp.cumsum(onehot, axis=0)
    counts = csum[-1]
    rank = jnp.take_along_axis(csum, flat_e[:, None], axis=1)[:, 0] - 1
    padded = (counts + tm - 1) // tm * tm
    pend = jnp.cumsum(padded)
    pstart = pend - padded
    dest = pstart[flat_e] + rank
    P = _round_up(TOP_K * n_tok, tm) + E * tm
    row_src = jnp.zeros((P,), jnp.int32).at[dest].set(flat_t)
    n_tiles = (pend[-1] // tm).astype(jnp.int32).reshape(1)
    tile_start = jnp.arange(P // tm, dtype=jnp.int32) * tm
    tile_expert = jnp.minimum(jnp.searchsorted(pend, tile_start, side='right'), E - 1).astype(jnp.int32)
    xs = jnp.take(x1b, row_src, axis=0)
    ys = _moe_experts(xs, tile_expert, n_tiles, w1, w3, w2, layer)
    pos = jnp.concatenate([dest.reshape(n_tok, TOP_K).astype(jnp.int32),
                           jnp.zeros((Mp - n_tok, TOP_K), jnp.int32)], axis=0)
    pair_g = jnp.concatenate([pair_g, jnp.zeros((Mp - n_tok, TOP_K), F32)], axis=0)
    gate_rows = [jnp.broadcast_to(pair_g[:, k:k + 1], (Mp, LANES)) for k in range(TOP_K)]
    return ys, pos, gate_rows


def _ln_combine_body(pos_ref, x_ref, w0_ref, w1_ref, g_ref, b_ref, ys_hbm, yf_ref, yb_ref, buf0, buf1, sem0, sem1,
                     *, tm):
    def row_copy(k, r, buf, sem):
        return pltpu.make_async_copy(ys_hbm.at[pl.ds(pos_ref[0, 0, TOP_K * r + k], 1)], buf.at[pl.ds(r, 1)], sem)

    def issue(r, carry):
        row_copy(0, r, buf0, sem0).start()
        row_copy(1, r, buf1, sem1).start()
        return carry

    lax.fori_loop(0, tm, issue, 0)
    pltpu.make_async_copy(ys_hbm.at[pl.ds(0, tm)], buf0, sem0).wait()
    pltpu.make_async_copy(ys_hbm.at[pl.ds(0, tm)], buf1, sem1).wait()
    d = x_ref.shape[1]
    f = _lanes_to(w0_ref[...], d) * buf0[...] + _lanes_to(w1_ref[...], d) * buf1[...]
    x = DN_ALPHA * x_ref[...] + f
    mu = jnp.mean(x, axis=-1, keepdims=True)
    xc = x - mu
    var = jnp.mean(xc * xc, axis=-1, keepdims=True)
    y = xc * lax.rsqrt(var + LN_EPS) * g_ref[...] + b_ref[...]
    yf_ref[...] = y
    yb_ref[...] = y.astype(BF16)


def _ln_combine(x1, ys, pos, gate_rows, g, b):
    M, D = x1.shape
    tm = _pick(M, (256, 128, 64, 32, 16))
    n = M // tm
    row = pl.BlockSpec((tm, D), lambda i: (i, 0))
    vec = pl.BlockSpec((1, D), lambda i: (0, 0))
    gcol = pl.BlockSpec((tm, LANES), lambda i: (i, 0))
    return pl.pallas_call(
        functools.partial(_ln_combine_body, tm=tm),
        grid=(n,),
        in_specs=[pl.BlockSpec((1, 1, TOP_K * tm), lambda i: (i, 0, 0), memory_space=pltpu.SMEM),
                  row, gcol, gcol, vec, vec, pl.BlockSpec(memory_space=pl.ANY)],
        out_specs=[row, row],
        out_shape=[jax.ShapeDtypeStruct((M, D), F32), jax.ShapeDtypeStruct((M, D), BF16)],
        scratch_shapes=[pltpu.VMEM((tm, D), F32), pltpu.VMEM((tm, D), F32),
                        pltpu.SemaphoreType.DMA(()), pltpu.SemaphoreType.DMA(())],
        compiler_params=_params(("arbitrary",)),
        name="ln2_combine",
    )(pos.reshape(n, 1, TOP_K * tm), x1, gate_rows[0], gate_rows[1], g.reshape(1, D), b.reshape(1, D), ys)


def _softmax_rows(s, bias_ref, m_ref, l_ref, rs):
    tq = s.shape[0]
    m_all = m_ref[...]
    l_all = l_ref[...]
    ps, alphas, ms, ls = [], [], [], []
    for r in range(tq // rs):
        rows = slice(r * rs, (r + 1) * rs)
        sb = s[rows, :]
        if bias_ref is not None:
            sb = sb + bias_ref[rows, :]
        m_prev = m_all[rows, :]
        m_new = jnp.maximum(m_prev, jnp.max(sb, axis=1, keepdims=True))
        alpha = jnp.exp2(m_prev - m_new)
        p = jnp.exp2(sb - _lanes_to(m_new, sb.shape[1]))
        ls.append(alpha * l_all[rows, :] + jnp.sum(p, axis=1, keepdims=True))
        ms.append(m_new)
        ps.append(p.astype(BF16))
        alphas.append(alpha)
    cat = (lambda xs: xs[0]) if len(ps) == 1 else (lambda xs: jnp.concatenate(xs, axis=0))
    m_ref[...] = cat(ms)
    l_ref[...] = cat(ls)
    return cat(ps), cat(alphas)


def _lanes_to(x, width):
    return x if width == LANES else jnp.concatenate([x] * (width // LANES), axis=1)


def _flash_init(ki, state):
    @pl.when(ki == 0)
    def _():
        for m_ref, l_ref, acc_ref in state:
            m_ref[...] = jnp.full(m_ref.shape, NEG_INF, F32)
            l_ref[...] = jnp.zeros(l_ref.shape, F32)
            acc_ref[...] = jnp.zeros(acc_ref.shape, F32)


def _flash_state(scratch, n):
    return [tuple(scratch[3 * i:3 * i + 3]) for i in range(n)]


def _flash_mla_body(ki_ref, last_ref, bias_tab_ref, q_ref, k_ref, v_ref, bias_ref, o_ref, *scratch, G, rs):
    step = pl.program_id(2)
    state = _flash_state(scratch, G)
    _flash_init(ki_ref[step], state)
    nt = (((1,), (1,)), ((), ()))

    def update(bias):
        for g in range(G):
            m_ref, l_ref, acc_ref = state[g]
            q = q_ref[:, g * MLA_HEAD_PAD:(g + 1) * MLA_HEAD_PAD]
            k = k_ref[:, g * MLA_HEAD_PAD:(g + 1) * MLA_HEAD_PAD]
            s = lax.dot_general(q, k, nt, preferred_element_type=F32)
            p, alpha = _softmax_rows(s, bias, m_ref, l_ref, rs)
            pv = jnp.dot(p, v_ref[:, g * MLA_V:(g + 1) * MLA_V], preferred_element_type=F32)
            acc_ref[...] = _lanes_to(alpha, pv.shape[1]) * acc_ref[...] + pv

    pl.when(bias_tab_ref[step] == 1)(lambda: update(bias_ref))
    pl.when(bias_tab_ref[step] == 0)(lambda: update(None))

    @pl.when(last_ref[step] == 1)
    def _():
        for g in range(G):
            _, l_ref, acc_ref = state[g]
            o_ref[:, g * MLA_V:(g + 1) * MLA_V] = (acc_ref[...] / _lanes_to(l_ref[...], MLA_V)).astype(BF16)


def _flash_diff_body(ki_ref, last_ref, bias_tab_ref, q_ref, k_ref, v_ref, bias_ref, lq1_ref, lk1_ref, lq2_ref,
                     lk2_ref, gs_ref, o_ref, *scratch, G, rs, lam_init):
    step = pl.program_id(2)
    hd = 2 * LANES
    state = _flash_state(scratch, 2 * G)
    _flash_init(ki_ref[step], state)
    nt = (((1,), (1,)), ((), ()))

    def update(bias):
        for g in range(G):
            v = v_ref[:, g * hd:(g + 1) * hd]
            for c in range(2):
                m_ref, l_ref, acc_ref = state[2 * g + c]
                col = slice(g * hd + c * LANES, g * hd + (c + 1) * LANES)
                s = lax.dot_general(q_ref[:, col], k_ref[:, col], nt, preferred_element_type=F32)
                p, alpha = _softmax_rows(s, bias, m_ref, l_ref, rs)
                pv = jnp.dot(p, v, preferred_element_type=F32)
                acc_ref[...] = _lanes_to(alpha, pv.shape[1]) * acc_ref[...] + pv

    pl.when(bias_tab_ref[step] == 1)(lambda: update(bias_ref))
    pl.when(bias_tab_ref[step] == 0)(lambda: update(None))

    @pl.when(last_ref[step] == 1)
    def _():
        lam = (jnp.exp(jnp.sum(lq1_ref[...] * lk1_ref[...], axis=-1, keepdims=True))
               - jnp.exp(jnp.sum(lq2_ref[...] * lk2_ref[...], axis=-1, keepdims=True)) + lam_init)
        for g in range(G):
            (_, l0_ref, acc0_ref), (_, l1_ref, acc1_ref) = state[2 * g], state[2 * g + 1]
            o0 = acc0_ref[...] / _lanes_to(l0_ref[...], hd)
            o1 = acc1_ref[...] / _lanes_to(l1_ref[...], hd)
            o = o0 - lam * o1
            o = o * lax.rsqrt(jnp.mean(o * o, axis=-1, keepdims=True) + RMS_EPS) * gs_ref[...]
            o_ref[:, g * hd:(g + 1) * hd] = (o * (1.0 - lam_init)).astype(BF16)


def _flash(kind, q, k, v, *, n_batch, sq, sk, tq, tk, q_row0, n_heads, q_pos0, n_valid_k, extras=(),
           lam_init=0.0):
    if kind == "mla":
        dqk, dv, n_state = MLA_HEAD_PAD, MLA_V, 1
    else:
        dqk, dv, n_state = 2 * LANES, 2 * LANES, 2
    G = _pick(n_heads, (4, 2, 1))
    nq, nk = sq // tq, sk // tk

    def visible(a, b):
        return (b * tk) // CHUNK <= (q_pos0 + a * tq + tq - 1) // CHUNK and b * tk < n_valid_k

    def full(a, b):
        return (b * tk + tk - 1) // CHUNK <= (q_pos0 + a * tq) // CHUNK and b * tk + tk <= n_valid_k

    pairs = [(a, b) for a in range(nq) for b in range(nk) if visible(a, b)]
    partial = [(a, b) for a, b in pairs if not full(a, b)]
    assert partial and len({(q_pos0 + a * tq - b * tk, min(n_valid_k - b * tk, tk)) for a, b in partial}) == 1
    a0, b0 = partial[0]
    q_pos = q_pos0 + a0 * tq + lax.broadcasted_iota(jnp.int32, (tq, tk), 0)
    k_pos = b0 * tk + lax.broadcasted_iota(jnp.int32, (tq, tk), 1)
    bias = jnp.where(jnp.logical_and(k_pos // CHUNK <= q_pos // CHUNK, k_pos < n_valid_k), 0.0, NEG_INF
                     ).astype(F32)
    for n, (a, b) in enumerate(pairs):
        assert (b == 0) == (n == 0 or pairs[n - 1][0] != a)
    qi_tab = jnp.asarray([a for a, _ in pairs], jnp.int32)
    ki_tab = jnp.asarray([b for _, b in pairs], jnp.int32)
    last_tab = jnp.asarray([1 if (n + 1 == len(pairs) or pairs[n + 1][0] != a) else 0
                            for n, (a, _) in enumerate(pairs)], jnp.int32)
    bias_tab = jnp.asarray([0 if full(a, b) else 1 for a, b in pairs], jnp.int32)
    qb0 = q_row0 // tq
    assert q_row0 % tq == 0
    in_specs = [pl.BlockSpec((tq, G * dqk), lambda b, h, s, qt, kt, lt, bt: (qb0 + b * nq + qt[s], h)),
                pl.BlockSpec((tk, G * dqk), lambda b, h, s, qt, kt, lt, bt: (b * nk + kt[s], h)),
                pl.BlockSpec((tk, G * dv), lambda b, h, s, qt, kt, lt, bt: (b * nk + kt[s], h)),
                pl.BlockSpec((tq, tk), lambda b, h, s, qt, kt, lt, bt: (0, 0))]
    operands = [q, k, v, bias]
    rs = _pick(tq, (64, 32, 16))
    if kind == "mla":
        inner = functools.partial(_flash_mla_body, G=G, rs=rs)
    else:
        inner = functools.partial(_flash_diff_body, G=G, rs=rs, lam_init=lam_init)
        for e in extras:
            in_specs.append(pl.BlockSpec(e.shape, lambda b, h, s, qt, kt, lt, bt: (0, 0)))
            operands.append(e)

    def body(qi_ref, ki_ref, last_ref, bias_tab_ref, *refs):
        inner(ki_ref, last_ref, bias_tab_ref, *refs)

    return pl.pallas_call(
        body,
        grid_spec=pltpu.PrefetchScalarGridSpec(
            num_scalar_prefetch=4,
            grid=(n_batch, n_heads // G, len(pairs)),
            in_specs=in_specs,
            out_specs=pl.BlockSpec((tq, G * dv), lambda b, h, s, qt, kt, lt, bt: (b * nq + qt[s], h)),
            scratch_shapes=[pltpu.VMEM((tq, LANES), F32), pltpu.VMEM((tq, LANES), F32),
                            pltpu.VMEM((tq, dv), F32)] * (G * n_state)),
        out_shape=jax.ShapeDtypeStruct((n_batch * sq, n_heads * dv), BF16),
        compiler_params=_params(("arbitrary", "arbitrary", "arbitrary")),
        name="flash_" + kind,
    )(qi_tab, ki_tab, last_tab, bias_tab, *operands)


def _pool_body(halo_ref, x_ref, o_ref, e_ref, *, tm, tiles_per_seq, n_hist, halo):
    i = pl.program_id(0)
    first = (i % tiles_per_seq) == 0
    hv = halo_ref[...]
    if n_hist == 0:
        hv = jnp.where(first, jnp.zeros_like(hv), hv)
    e_ref[0:halo, :] = hv
    e_ref[halo:halo + tm, :] = x_ref[...]
    t_glob = (i % tiles_per_seq) * tm + lax.broadcasted_iota(jnp.int32, (tm, 1), 0)
    avail = (t_glob + 1 + n_hist).astype(F32)
    D = x_ref.shape[1]
    gw = D // len(POOL_WINDOWS)
    for g, win in enumerate(POOL_WINDOWS):
        cols = slice(g * gw, (g + 1) * gw)
        s = e_ref[halo:halo + tm, cols]
        xg = s
        for kk in range(1, win):
            s = s + e_ref[halo - kk:halo - kk + tm, cols]
        cnt = jnp.minimum(avail, jnp.float32(win))
        o_ref[:, cols] = (s / cnt - xg).astype(BF16)


def _pool_premix(x, halo_arr, *, n_tiles, tm, tiles_per_seq, n_hist, halo_index):
    D = x.shape[1]
    halo = max(POOL_WINDOWS)
    return pl.pallas_call(
        functools.partial(_pool_body, tm=tm, tiles_per_seq=tiles_per_seq, n_hist=n_hist, halo=halo),
        grid=(n_tiles,),
        in_specs=[pl.BlockSpec((halo, D), halo_index),
                  pl.BlockSpec((tm, D), lambda i: (i, 0))],
        out_specs=pl.BlockSpec((tm, D), lambda i: (i, 0)),
        out_shape=jax.ShapeDtypeStruct((n_tiles * tm, D), BF16),
        scratch_shapes=[pltpu.VMEM((halo + tm, D), F32)],
        compiler_params=_params(("arbitrary",)),
        name="pool_premix",
    )(halo_arr, x)


def _conv_body(halo_ref, u_ref, w_ref, bdw_ref, g_ref, b_ref, o_ref, e_ref, c_ref, *, tm, tiles_per_seq,
               zero_first, halo):
    i = pl.program_id(0)
    hv = halo_ref[...]
    if zero_first:
        hv = jnp.where((i % tiles_per_seq) == 0, jnp.zeros_like(hv), hv)
    e_ref[0:halo, :] = hv
    e_ref[halo:halo + tm, :] = u_ref[...]
    D = u_ref.shape[1]
    cw = _pick(D, (512, 256, 128))
    off = halo - (CONV_WIDTH - 1)
    for cc in range(D // cw):
        cols = slice(cc * cw, (cc + 1) * cw)
        acc = e_ref[off:off + tm, cols] * w_ref[0:1, cols]
        for k in range(1, CONV_WIDTH):
            acc = acc + e_ref[off + k:off + k + tm, cols] * w_ref[k:k + 1, cols]
        c_ref[:, cols] = acc + bdw_ref[:, cols]
    c = c_ref[...]
    mu = jnp.mean(c, axis=-1, keepdims=True)
    cc0 = c - mu
    var = jnp.mean(cc0 * cc0, axis=-1, keepdims=True)
    y = cc0 * lax.rsqrt(var + LN_EPS) * g_ref[...] + b_ref[...]
    o_ref[...] = (y * jax.nn.sigmoid(y)).astype(BF16)


def _conv_dw(u, halo_arr, w_dw, b_dw, ln_g, ln_b, *, n_tiles, tm, tiles_per_seq, zero_first, halo_index):
    D = u.shape[1]
    halo = _round_up(CONV_WIDTH - 1, 8)
    kw = w_dw.shape[0]
    vec = pl.BlockSpec((1, D), lambda i: (0, 0))
    return pl.pallas_call(
        functools.partial(_conv_body, tm=tm, tiles_per_seq=tiles_per_seq, zero_first=zero_first, halo=halo),
        grid=(n_tiles,),
        in_specs=[pl.BlockSpec((halo, D), halo_index),
                  pl.BlockSpec((tm, D), lambda i: (i, 0)),
                  pl.BlockSpec((kw, D), lambda i: (0, 0)), vec, vec, vec],
        out_specs=pl.BlockSpec((tm, D), lambda i: (i, 0)),
        out_shape=jax.ShapeDtypeStruct((n_tiles * tm, D), BF16),
        scratch_shapes=[pltpu.VMEM((halo + tm, D), F32), pltpu.VMEM((tm, D), F32)],
        compiler_params=_params(("arbitrary",)),
        name="conv_dw",
    )(halo_arr, u, w_dw, b_dw.reshape(1, D), ln_g.reshape(1, D), ln_b.reshape(1, D))


def _rope_tables(pos, half, width):
    inv_freq = ROPE_THETA ** (-jnp.arange(half, dtype=F32) / half)
    ang = pos.astype(F32)[:, None] * inv_freq[None, :]
    cos, sin = jnp.cos(ang), jnp.sin(ang)
    n = pos.shape[0]
    c = jnp.concatenate([cos, cos, jnp.ones((n, width - 2 * half), F32)], axis=1)
    s = jnp.concatenate([-sin, sin, jnp.zeros((n, width - 2 * half), F32)], axis=1)
    return c, s


def _stack_rows(prompt_part, sample_part, total_rows):
    pad = total_rows - prompt_part.shape[0] - sample_part.shape[0]
    return jnp.concatenate([prompt_part, sample_part,
                            jnp.zeros((pad, prompt_part.shape[1]), prompt_part.dtype)], axis=0)


def _mla_mixer(xb, pos_rows, past_c, past_pe, w_dq, g_q, w_uq, w_dkv, g_kv, w_ukv, w_o, dims):
    Tp, Ts, Mp = dims
    D = xb.shape[1]
    H = MLA_HEADS
    r = MLA_KV_RANK
    cm, sm = _rope_tables(pos_rows, MLA_ROPE // 2, LANES)

    (qlat,) = _fused_matmul(xb, w_dq, _ep_rms, tn=MLA_Q_RANK, n_cols=MLA_Q_RANK,
                            aux=[(g_q.reshape(1, -1), 'col')], outs=[(MLA_Q_RANK, BF16)], name="mla_dq")
    w_dkv_p = jnp.concatenate([w_dkv, jnp.zeros((D, LANES - MLA_ROPE), F32)], axis=1)
    c_f, c_b, pe_f, pe_b = _fused_matmul(
        xb, w_dkv_p, _ep_mla_dkv, tn=r + LANES, n_cols=r + LANES,
        aux=[(g_kv.reshape(1, r), 'full'), (cm, 'row'), (sm, 'row')],
        outs=[(r, F32), (r, BF16), (LANES, F32), (LANES, BF16)], name="mla_dkv")

    w_uq_h = w_uq.reshape(MLA_Q_RANK, H, MLA_NOPE + MLA_ROPE)
    w_uq_p = jnp.concatenate([w_uq_h, jnp.zeros((MLA_Q_RANK, H, MLA_HEAD_PAD - MLA_NOPE - MLA_ROPE), F32)],
                             axis=2).reshape(MLA_Q_RANK, H * MLA_HEAD_PAD)
    hpt = _pick(H, (4, 2, 1))
    scale = (MLA_NOPE + MLA_ROPE) ** -0.5 * LOG2E
    (q_b,) = _fused_matmul(qlat, w_uq_p, functools.partial(_ep_mla_q, scale=scale, heads_per_tile=hpt),
                           tn=hpt * MLA_HEAD_PAD, n_cols=H * MLA_HEAD_PAD,
                           aux=[(cm, 'row'), (sm, 'row')], outs=[(hpt * MLA_HEAD_PAD, BF16)], name="mla_uq")

    w_ukv_h = w_ukv.reshape(r, H, MLA_NOPE + MLA_V)
    w_uk = w_ukv_h[:, :, :MLA_NOPE].reshape(r, H * MLA_NOPE)
    w_uv = w_ukv_h[:, :, MLA_NOPE:].reshape(r, H * MLA_V)
    hpk = _pick(H, (8, 4, 2, 1))

    def kv_up(c_rows, pe_rows):
        (kf,) = _fused_matmul(c_rows, w_uk, functools.partial(_ep_mla_k, heads_per_tile=hpk),
                              tn=hpk * MLA_NOPE, n_cols=H * MLA_NOPE, aux=[(pe_rows, 'row')],
                              outs=[(hpk * MLA_HEAD_PAD, BF16)], name="mla_uk")
        (vv,) = _fused_matmul(c_rows, w_uv, _ep_plain, tn=_pick(H * MLA_V, (1024, 512, 256, 128)),
                              n_cols=H * MLA_V, outs=[(_pick(H * MLA_V, (1024, 512, 256, 128)), BF16)],
                              name="mla_uv")
        return kf, vv

    kf_p, v_p = kv_up(c_b[:Tp], pe_b[:Tp])
    tq = _pick(SEQ, (SEQ_TQ, 256, 128))
    o_p = _flash("mla", q_b, kf_p, v_p, n_batch=BATCH, sq=SEQ, sk=SEQ, tq=tq, tk=tq, q_row0=0,
                 n_heads=H, q_pos0=0, n_valid_k=SEQ)

    lk = _round_up(PAST_LEN + DEC_SEQ, LANES)
    padk = lk - PAST_LEN - DEC_SEQ
    c_s = jnp.concatenate([past_c.astype(BF16), c_b[Tp:Tp + Ts].reshape(DEC_BATCH, DEC_SEQ, r),
                           jnp.zeros((DEC_BATCH, padk, r), BF16)], axis=1).reshape(DEC_BATCH * lk, r)
    past_pe_p = jnp.concatenate([past_pe, jnp.zeros(past_pe.shape[:2] + (LANES - MLA_ROPE,), F32)], axis=2)
    pe_s = jnp.concatenate([past_pe_p.astype(BF16), pe_b[Tp:Tp + Ts].reshape(DEC_BATCH, DEC_SEQ, LANES),
                            jnp.zeros((DEC_BATCH, padk, LANES), BF16)], axis=1).reshape(DEC_BATCH * lk, LANES)
    kf_s, v_s = kv_up(c_s, pe_s)
    o_s = _flash("mla", q_b, kf_s, v_s, n_batch=DEC_BATCH, sq=DEC_SEQ, sk=lk, tq=DEC_SEQ, tk=lk, q_row0=Tp,
                 n_heads=H, q_pos0=PAST_LEN, n_valid_k=PAST_LEN + DEC_SEQ)

    o_all = _stack_rows(o_p, o_s, Mp)
    tn_o = _pick(D, (512, 256, 128))
    (h,) = _fused_matmul(o_all, w_o, _ep_plain, tn=tn_o, n_cols=D, outs=[(tn_o, F32)], name="mla_wo")
    new_p = (c_f[:Tp].reshape(BATCH, SEQ, r), pe_f[:Tp, :MLA_ROPE].reshape(BATCH, SEQ, MLA_ROPE))
    new_s = (c_f[Tp:Tp + Ts].reshape(DEC_BATCH, DEC_SEQ, r),
             pe_f[Tp:Tp + Ts, :MLA_ROPE].reshape(DEC_BATCH, DEC_SEQ, MLA_ROPE))
    return h, new_p, new_s


def _pool_mixer(x, hist, w_pool, scale, dims):
    Tp, Ts, Mp = dims
    D = x.shape[1]
    halo = max(POOL_WINDOWS)
    n_hist = halo - 1
    tm = _pick(SEQ, (POOL_TM, 128, 64, 32, 16))
    tps = SEQ // tm
    bpt = tm // halo
    d_p = _pool_premix(x, x, n_tiles=Tp // tm, tm=tm, tiles_per_seq=tps, n_hist=0,
                       halo_index=lambda i: (jnp.maximum(i * bpt - 1, 0), 0))
    xs = x[Tp:Tp + Ts].reshape(DEC_BATCH, DEC_SEQ, D)
    halo_s = jnp.concatenate([jnp.zeros((DEC_BATCH, halo - n_hist, D), F32), hist], axis=1
                             ).reshape(DEC_BATCH * halo, D)
    d_s = _pool_premix(xs.reshape(Ts, D), halo_s, n_tiles=DEC_BATCH, tm=DEC_SEQ, tiles_per_seq=1,
                       n_hist=n_hist, halo_index=lambda i: (i, 0))
    d_all = _stack_rows(d_p, d_s, Mp)
    gw = D // len(POOL_WINDOWS)
    (h,) = _fused_matmul(d_all, w_pool, _ep_scale_cols, tn=gw, n_cols=D, x_kblock=True, w_lead_from_j=True,
                         aux=[(scale.reshape(1, D), 'col')], outs=[(gw, F32)], name="pool_mm")
    new_p = x[:Tp].reshape(BATCH, SEQ, D)[:, SEQ - n_hist:]
    new_s = jnp.concatenate([hist, xs], axis=1)[:, -n_hist:]
    return h, new_p, new_s


def _conv_mixer(x, xb, hist, w_pw1, b_pw1, w_dw, b_dw, ln_g, ln_b, w_pw2, b_pw2, dims):
    Tp, Ts, Mp = dims
    D = x.shape[1]
    n_hist = CONV_WIDTH - 1
    halo = _round_up(n_hist, 8)
    u = _glu_matmul(xb, w_pw1, b_pw1.reshape(1, -1))
    tm = _pick(SEQ, (CONV_TM, 32, 16))
    assert tm % halo == 0 or halo % tm == 0
    tps = SEQ // tm
    bpt = tm // halo
    a_p = _conv_dw(u, u, w_dw, b_dw, ln_g, ln_b, n_tiles=Tp // tm, tm=tm, tiles_per_seq=tps, zero_first=True,
                   halo_index=lambda i: (jnp.maximum(i * bpt - 1, 0), 0))
    us = u[Tp:Tp + Ts].reshape(DEC_BATCH, DEC_SEQ, D)
    halo_s = jnp.concatenate([jnp.zeros((DEC_BATCH, halo - n_hist, D), F32), hist], axis=1
                             ).reshape(DEC_BATCH * halo, D)
    a_s = _conv_dw(us.reshape(Ts, D), halo_s, w_dw, b_dw, ln_g, ln_b, n_tiles=DEC_BATCH, tm=DEC_SEQ,
                   tiles_per_seq=1, zero_first=False, halo_index=lambda i: (i, 0))
    a_all = _stack_rows(a_p, a_s, Mp)
    tn = _pick(D, (512, 256, 128))
    (h,) = _fused_matmul(a_all, w_pw2, _ep_bias, tn=tn, n_cols=D, aux=[(b_pw2.reshape(1, D), 'col')],
                         outs=[(tn, F32)], name="conv_pw2")
    new_p = u[:Tp].reshape(BATCH, SEQ, D)[:, SEQ - n_hist:]
    new_s = jnp.concatenate([hist, us], axis=1)[:, -n_hist:]
    return h, new_p, new_s


def _diff_mixer(xb, pos_rows, past_k, past_v, w_qkv, lq1, lk1, lq2, lk2, g_sub, w_o, lam_init, dims):
    Tp, Ts, Mp = dims
    D = xb.shape[1]
    H = DIFF_HEADS
    hd = D // H
    assert hd == 2 * LANES
    cd, sd = _rope_tables(pos_rows, LANES // 2, LANES)
    scale = (hd // 2) ** -0.5 * LOG2E
    tn = _pick(D, (512, 256, 128))
    ns = tn // LANES
    tabs = [(cd, 'row'), (sd, 'row')]
    (q_b,) = _fused_matmul(xb, w_qkv, functools.partial(_ep_rope_heads, scale=scale, n_slabs=ns, want_f32=False),
                           tn=tn, n_cols=D, col0=0, aux=tabs, outs=[(tn, BF16)], name="diff_q")
    ep_k = functools.partial(_ep_rope_heads, scale=1.0, n_slabs=ns, want_f32=True)
    kv_outs = [(tn, F32), (tn, BF16)]
    kp_f, kp_b = _fused_matmul(xb, w_qkv, ep_k, tn=tn, n_cols=D, col0=D, aux=tabs, outs=kv_outs,
                               row0=0, n_rows=Tp, name="diff_k")
    ks_f, ks_b = _fused_matmul(xb, w_qkv, ep_k, tn=tn, n_cols=D, col0=D, aux=tabs, outs=kv_outs,
                               row0=Tp, n_rows=Ts, name="diff_k_s")
    vp_f, vp_b = _fused_matmul(xb, w_qkv, _ep_plain, tn=tn, n_cols=D, col0=2 * D, outs=kv_outs,
                               row0=0, n_rows=Tp, name="diff_v")
    vs_f, vs_b = _fused_matmul(xb, w_qkv, _ep_plain, tn=tn, n_cols=D, col0=2 * D, outs=kv_outs,
                               row0=Tp, n_rows=Ts, name="diff_v_s")
    extras = [lq1.reshape(1, -1), lk1.reshape(1, -1), lq2.reshape(1, -1), lk2.reshape(1, -1),
              g_sub.reshape(1, -1)]
    tq = _pick(SEQ, (SEQ_TQ, 256, 128))
    o_p = _flash("diff", q_b, kp_b, vp_b, n_batch=BATCH, sq=SEQ, sk=SEQ, tq=tq, tk=tq, q_row0=0, n_heads=H,
                 q_pos0=0, n_valid_k=SEQ, extras=extras, lam_init=lam_init)
    lk = _round_up(PAST_LEN + DEC_SEQ, LANES)
    padk = lk - PAST_LEN - DEC_SEQ

    def with_cache(past, new_b):
        return jnp.concatenate([past.reshape(DEC_BATCH, PAST_LEN, D).astype(BF16),
                                new_b.reshape(DEC_BATCH, DEC_SEQ, D),
                                jnp.zeros((DEC_BATCH, padk, D), BF16)], axis=1).reshape(DEC_BATCH * lk, D)

    o_s = _flash("diff", q_b, with_cache(past_k, ks_b), with_cache(past_v, vs_b), n_batch=DEC_BATCH, sq=DEC_SEQ,
                 sk=lk, tq=DEC_SEQ, tk=lk, q_row0=Tp, n_heads=H, q_pos0=PAST_LEN,
                 n_valid_k=PAST_LEN + DEC_SEQ, extras=extras, lam_init=lam_init)
    o_all = _stack_rows(o_p, o_s, Mp)
    (h,) = _fused_matmul(o_all, w_o, _ep_plain, tn=tn, n_cols=D, outs=[(tn, F32)], name="diff_wo")
    new_p = (kp_f.reshape(BATCH, SEQ, H, 2, hd // 2), vp_f.reshape(BATCH, SEQ, H, hd))
    new_s = (ks_f.reshape(DEC_BATCH, DEC_SEQ, H, 2, hd // 2), vs_f.reshape(DEC_BATCH, DEC_SEQ, H, hd))
    return h, new_p, new_s


def kernel(x_prompt, x_sample, cache_mla_ckv, cache_mla_kpe, state_pool, state_conv, cache_diff_k, cache_diff_v, p_prompt, p_sample, mla_w_dq, mla_g_q, mla_w_uq, mla_w_dkv, mla_g_kv, mla_w_ukv, mla_w_o, pool_w, pool_scale, conv_w_pw1, conv_b_pw1, conv_w_dw, conv_b_dw, conv_ln_g, conv_ln_b, conv_w_pw2, conv_b_pw2, diff_w_qkv, diff_lam_q1, diff_lam_k1, diff_lam_q2, diff_lam_k2, diff_g_sub, diff_w_o, router_w, router_b, moe_w1, moe_w3, moe_w2, ln1_g, ln1_b, ln2_g, ln2_b, ple_w, ple_gate):
    D = D_MODEL
    Tp, Ts = BATCH * SEQ, DEC_BATCH * DEC_SEQ
    T = Tp + Ts
    Mp = _round_up(T, ROW_PAD)
    dims = (Tp, Ts, Mp)

    x = _stack_rows(x_prompt.reshape(Tp, D), x_sample.reshape(Ts, D), Mp)
    xb = x.astype(BF16)
    p_all = jnp.concatenate([p_prompt.reshape(DEPTH, Tp, PLE_DIM), p_sample.reshape(DEPTH, Ts, PLE_DIM),
                             jnp.zeros((DEPTH, Mp - T, PLE_DIM), F32)], axis=1)
    pos_rows = jnp.concatenate([jnp.tile(jnp.arange(SEQ, dtype=jnp.int32), BATCH),
                                jnp.tile(PAST_LEN + jnp.arange(DEC_SEQ, dtype=jnp.int32), DEC_BATCH),
                                jnp.zeros((Mp - T,), jnp.int32)])

    router = (router_w.T, router_b.reshape(N_EXPERTS, 1))

    new_p = {k: [] for k in ('mla_ckv', 'mla_kpe', 'pool', 'conv', 'diff_k', 'diff_v')}
    new_s = {k: [] for k in new_p}
    for i in range(DEPTH):
        kind, j = i % N_MIXERS, i // N_MIXERS
        if kind == 0:
            h, (cp, pp), (cs, ps) = _mla_mixer(
                xb, pos_rows, cache_mla_ckv[j], cache_mla_kpe[j], mla_w_dq[j], mla_g_q[j], mla_w_uq[j],
                mla_w_dkv[j], mla_g_kv[j], mla_w_ukv[j], mla_w_o[j], dims)
            new_p['mla_ckv'].append(cp); new_p['mla_kpe'].append(pp)
            new_s['mla_ckv'].append(cs); new_s['mla_kpe'].append(ps)
        elif kind == 1:
            h, sp, ss = _pool_mixer(x, state_pool[j], pool_w[j], pool_scale[j], dims)
            new_p['pool'].append(sp); new_s['pool'].append(ss)
        elif kind == 2:
            h, sp, ss = _conv_mixer(x, xb, state_conv[j], conv_w_pw1[j], conv_b_pw1[j], conv_w_dw[j],
                                    conv_b_dw[j], conv_ln_g[j], conv_ln_b[j], conv_w_pw2[j], conv_b_pw2[j], dims)
            new_p['conv'].append(sp); new_s['conv'].append(ss)
        else:
            lam_init = 0.8 - 0.6 * math.exp(-0.3 * i)
            h, (kp, vp), (ks, vs) = _diff_mixer(
                xb, pos_rows, cache_diff_k[j], cache_diff_v[j], diff_w_qkv[j], diff_lam_q1[j], diff_lam_k1[j],
                diff_lam_q2[j], diff_lam_k2[j], diff_g_sub[j], diff_w_o[j], lam_init, dims)
            new_p['diff_k'].append(kp); new_p['diff_v'].append(vp)
            new_s['diff_k'].append(ks); new_s['diff_v'].append(vs)

        x1, x1b, gate_sel = _ln_residual(x, h, ln1_g[i], ln1_b[i], router=router, name="ln1_route")
        ys, pos, gate_rows = _moe_layer(x1b, gate_sel, T, moe_w1, moe_w3, moe_w2, i)
        x2, x2b = _ln_combine(x1, ys, pos, gate_rows, ln2_g[i], ln2_b[i])
        tn = _pick(D, (512, 256, 128))
        ple = functools.partial(_fused_matmul, x2b, ple_gate, _ep_ple, tn=tn, n_cols=D, w_lead=(i,),
                                aux=[(x2, 'tile'), (p_all[i], 'row'), (ple_w, 'colw', (i,))])
        if i + 1 < DEPTH:
            x, xb = ple(outs=[(tn, F32), (tn, BF16)], name="ple")
        else:
            (y_p,) = ple(outs=[(tn, F32)], row0=0, n_rows=Tp, name="ple_out")
            (y_s,) = ple(outs=[(tn, F32)], row0=Tp, n_rows=Ts, name="ple_out_s")

    y_prompt = y_p.reshape(BATCH, SEQ, D)
    y_sample = y_s.reshape(DEC_BATCH, DEC_SEQ, D)
    order = ('mla_ckv', 'mla_kpe', 'pool', 'conv', 'diff_k', 'diff_v')
    return (y_prompt, y_sample) + tuple(jnp.stack(new_p[k]) for k in order) \
        + tuple(jnp.stack(new_s[k]) for k in order)
```

```python
import functools
import math

import jax
import jax.numpy as jnp
from jax import lax
from jax.experimental import pallas as pl
from jax.experimental.pallas import tpu as pltpu

D_MODEL = 4096
BATCH = 8
SEQ = 2048
DEPTH = 4
DEC_BATCH = 8
DEC_SEQ = 16
PAST_LEN = 1024

CHUNK = 64
N_MIXERS = 4
ROPE_THETA = 10000.0
PLE_DIM = 256
LN_EPS = 1e-5
RMS_EPS = 1e-6
NEG_INF = -1e30
DN_ALPHA = (2 * DEPTH) ** 0.25
LOG2E = math.log2(math.e)

MLA_HEADS = 32
MLA_Q_RANK = 1024
MLA_KV_RANK = 512
MLA_NOPE = 128
MLA_ROPE = 64
MLA_V = 128

POOL_WINDOWS = (2, 4, 8, 16)
CONV_WIDTH = 31

DIFF_HEADS = 16

N_EXPERTS = 16
N_GROUPS = 4
TOP_K = 2
D_EXPERT = 1024

LANES = 128
MLA_HEAD_PAD = 2 * LANES
VMEM_LIMIT = 56 * 1024 * 1024
VMEM_BUDGET = 44 * 1024 * 1024
ROW_PAD = 256
MOE_TM = 256
SEQ_TQ = 512
CONV_TM = 64
POOL_TM = 256

BF16 = jnp.bfloat16
F32 = jnp.float32


def _round_up(n, m):
    return (n + m - 1) // m * m


def _pick(n, cands):
    for c in cands:
        if c <= n and n % c == 0:
            return c
    return n


def _params(sem):
    return pltpu.CompilerParams(dimension_semantics=sem, vmem_limit_bytes=VMEM_LIMIT)


def _fused_matmul(x, w, epilogue, *, tn, n_cols, col0=0, w_lead=(), aux=(), outs=(), x_kblock=None,
                  w_lead_from_j=False, row0=0, n_rows=None, name="mm"):
    M = x.shape[0] - row0 if n_rows is None else n_rows
    K = w.shape[-2]
    n_j = n_cols // tn
    assert n_cols % tn == 0 and col0 % tn == 0
    cb = col0 // tn
    nl = len(w.shape) - 2

    def vmem_bytes(tm):
        b = 2 * tm * K * 2 + 2 * K * tn * 4 + K * tn * 2 + tm * tn * 4
        for a in aux:
            if a[1] == 'row':
                b += 2 * tm * a[0].shape[1] * a[0].dtype.itemsize
            elif a[1] == 'tile':
                b += 2 * tm * tn * a[0].dtype.itemsize
            elif a[1] == 'colw':
                b += 2 * a[0].shape[-2] * tn * a[0].dtype.itemsize
        for width, dt in outs:
            b += 2 * tm * width * jnp.dtype(dt).itemsize
        return b

    tm = M
    for c in (1280, 1024, 768, 640, 512, 384, 256, 128, 64, 32, 16):
        if c <= M and M % c == 0 and row0 % c == 0 and vmem_bytes(c) <= VMEM_BUDGET:
            tm = c
            break
    assert M % tm == 0 and row0 % tm == 0
    n_i = M // tm
    rb = row0 // tm

    if x_kblock is None:
        x_spec = pl.BlockSpec((tm, K), lambda j, i: (rb + i, 0))
    else:
        x_spec = pl.BlockSpec((tm, K), lambda j, i: (rb + i, j))
    if w_lead_from_j:
        w_spec = pl.BlockSpec((None,) * nl + (K, tn), lambda j, i: (j,) * nl + (0, 0))
    else:
        w_spec = pl.BlockSpec((None,) * nl + (K, tn), lambda j, i: tuple(w_lead) + (0, cb + j))
    in_specs = [x_spec, w_spec]
    operands = [x, w]
    for a in aux:
        arr, kind = a[0], a[1]
        if kind == 'row':
            in_specs.append(pl.BlockSpec((tm, arr.shape[1]), lambda j, i: (rb + i, 0)))
        elif kind == 'tile':
            in_specs.append(pl.BlockSpec((tm, tn), lambda j, i: (rb + i, j)))
        elif kind == 'col':
            in_specs.append(pl.BlockSpec((1, tn), lambda j, i: (0, cb + j)))
        elif kind == 'full':
            in_specs.append(pl.BlockSpec(arr.shape, lambda j, i: (0, 0)))
        elif kind == 'colw':
            lead2 = tuple(a[2])
            ka = arr.shape[-2]
            in_specs.append(pl.BlockSpec((None,) * len(lead2) + (ka, tn),
                                         lambda j, i, lead2=lead2: lead2 + (0, cb + j)))
        else:
            raise ValueError(kind)
        operands.append(arr)
    out_specs = [pl.BlockSpec((tm, width), lambda j, i: (i, j)) for width, _ in outs]
    out_shape = [jax.ShapeDtypeStruct((M, width * n_j), dt) for width, dt in outs]
    na, no = len(aux), len(outs)

    def body(x_ref, w_ref, *rest):
        aux_refs = rest[:na]
        out_refs = rest[na:na + no]
        wb_ref = rest[na + no]

        @pl.when(pl.program_id(1) == 0)
        def _():
            wb_ref[...] = w_ref[...].astype(BF16)

        acc = jnp.dot(x_ref[...], wb_ref[...], preferred_element_type=F32)
        epilogue(acc, aux_refs, out_refs)

    res = pl.pallas_call(
        body,
        grid=(n_j, n_i),
        in_specs=in_specs,
        out_specs=out_specs,
        out_shape=out_shape,
        scratch_shapes=[pltpu.VMEM((K, tn), BF16)],
        compiler_params=_params(("arbitrary", "arbitrary")),
        name=name,
    )(*operands)
    return res


def _rope_slab(s, c, sn, half):
    if 2 * half == LANES:
        partner = pltpu.roll(s, half, axis=1)
    else:
        lane = lax.broadcasted_iota(jnp.int32, s.shape, 1)
        partner = jnp.where(lane < half, pltpu.roll(s, LANES - half, axis=1), pltpu.roll(s, half, axis=1))
    return s * c + partner * sn


def _ep_plain(acc, aux_refs, out_refs):
    for o in out_refs:
        o[...] = acc.astype(o.dtype)


def _ep_bias(acc, aux_refs, out_refs):
    out_refs[0][...] = (acc + aux_refs[0][...]).astype(out_refs[0].dtype)


def _ep_scale_cols(acc, aux_refs, out_refs):
    out_refs[0][...] = (acc * aux_refs[0][...]).astype(out_refs[0].dtype)


def _ep_rms(acc, aux_refs, out_refs):
    g = aux_refs[0][...]
    y = acc * lax.rsqrt(jnp.mean(acc * acc, axis=-1, keepdims=True) + RMS_EPS) * g
    out_refs[0][...] = y.astype(out_refs[0].dtype)


def _ep_mla_dkv(acc, aux_refs, out_refs):
    g_ref, c_ref, s_ref = aux_refs
    cf_ref, cb_ref, pf_ref, pb_ref = out_refs
    r = MLA_KV_RANK
    lat = acc[:, :r]
    cn = lat * lax.rsqrt(jnp.mean(lat * lat, axis=-1, keepdims=True) + RMS_EPS) * g_ref[...]
    cf_ref[...] = cn
    cb_ref[...] = cn.astype(BF16)
    pe = _rope_slab(acc[:, r:r + LANES], c_ref[...], s_ref[...], MLA_ROPE // 2)
    pf_ref[...] = pe
    pb_ref[...] = pe.astype(BF16)


def _ep_mla_q(acc, aux_refs, out_refs, *, scale, heads_per_tile):
    c = aux_refs[0][...]
    sn = aux_refs[1][...]
    o = out_refs[0]
    for h in range(heads_per_tile):
        a = h * MLA_HEAD_PAD
        o[:, a:a + LANES] = (acc[:, a:a + LANES] * scale).astype(BF16)
        pe = _rope_slab(acc[:, a + LANES:a + 2 * LANES], c, sn, MLA_ROPE // 2)
        o[:, a + LANES:a + 2 * LANES] = (pe * scale).astype(BF16)


def _ep_mla_k(acc, aux_refs, out_refs, *, heads_per_tile):
    pe = aux_refs[0][...]
    o = out_refs[0]
    for h in range(heads_per_tile):
        o[:, h * MLA_HEAD_PAD:h * MLA_HEAD_PAD + LANES] = acc[:, h * LANES:(h + 1) * LANES].astype(BF16)
        o[:, h * MLA_HEAD_PAD + LANES:(h + 1) * MLA_HEAD_PAD] = pe


def _ep_rope_heads(acc, aux_refs, out_refs, *, scale, n_slabs, want_f32):
    c = aux_refs[0][...]
    sn = aux_refs[1][...]
    for h in range(n_slabs):
        sl = slice(h * LANES, (h + 1) * LANES)
        y = _rope_slab(acc[:, sl], c, sn, LANES // 2)
        if want_f32:
            out_refs[0][:, sl] = y
            out_refs[1][:, sl] = y.astype(BF16)
        else:
            out_refs[0][:, sl] = (y * scale).astype(BF16)


def _ep_ple(acc, aux_refs, out_refs):
    x2_ref, p_ref, pw_ref = aux_refs
    p = p_ref[...]
    w = pw_ref[...]
    p_hi = p.astype(BF16)
    p_lo = (p - p_hi.astype(F32)).astype(BF16)
    w_hi = w.astype(BF16)
    w_lo = (w - w_hi.astype(F32)).astype(BF16)
    pw = (jnp.dot(p_hi, w_hi, preferred_element_type=F32) + jnp.dot(p_lo, w_hi, preferred_element_type=F32)
          + jnp.dot(p_hi, w_lo, preferred_element_type=F32))
    y = x2_ref[...] + pw * jax.nn.sigmoid(acc)
    out_refs[0][...] = y
    if len(out_refs) > 1:
        out_refs[1][...] = y.astype(BF16)


def _glu_matmul(x, w, b, *, name="glu"):
    M, K = x.shape
    D = w.shape[1] // 2
    tn = _pick(D, (512, 256, 128))
    n_j = D // tn
    tm = _pick(M, (1280, 1024, 640, 512, 256, 128, 64, 32, 16))
    while 2 * tm * K * 2 + 4 * K * tn * 4 + 2 * K * tn * 2 + 4 * tm * tn * 4 > VMEM_BUDGET and tm % 2 == 0 \
            and M % (tm // 2) == 0:
        tm //= 2
    n_i = M // tm

    def body(x_ref, wa_ref, wg_ref, ba_ref, bg_ref, o_ref, wab_ref, wgb_ref):
        @pl.when(pl.program_id(1) == 0)
        def _():
            wab_ref[...] = wa_ref[...].astype(BF16)
            wgb_ref[...] = wg_ref[...].astype(BF16)

        xv = x_ref[...]
        a = jnp.dot(xv, wab_ref[...], preferred_element_type=F32) + ba_ref[...]
        g = jnp.dot(xv, wgb_ref[...], preferred_element_type=F32) + bg_ref[...]
        o_ref[...] = a * jax.nn.sigmoid(g)

    return pl.pallas_call(
        body,
        grid=(n_j, n_i),
        in_specs=[pl.BlockSpec((tm, K), lambda j, i: (i, 0)),
                  pl.BlockSpec((K, tn), lambda j, i: (0, j)),
                  pl.BlockSpec((K, tn), lambda j, i: (0, n_j + j)),
                  pl.BlockSpec((1, tn), lambda j, i: (0, j)),
                  pl.BlockSpec((1, tn), lambda j, i: (0, n_j + j))],
        out_specs=pl.BlockSpec((tm, tn), lambda j, i: (i, j)),
        out_shape=jax.ShapeDtypeStruct((M, D), F32),
        scratch_shapes=[pltpu.VMEM((K, tn), BF16), pltpu.VMEM((K, tn), BF16)],
        compiler_params=_params(("arbitrary", "arbitrary")),
        name=name,
    )(x, w, w, b, b)


def _route_rows(scores, biased):
    epg = N_EXPERTS // N_GROUPS
    one = jnp.float32(1.0)
    zero = jnp.float32(0.0)
    rows = [biased[e:e + 1, :] for e in range(N_EXPERTS)]
    sel_in = []
    gscore = []
    for g in range(N_GROUPS):
        v = rows[g * epg:(g + 1) * epg]
        gs = None
        for k in range(epg):
            rank = None
            for m in range(epg):
                if m == k:
                    continue
                ahead = (v[m] > v[k]) if m > k else (v[m] >= v[k])
                t = jnp.where(ahead, one, zero)
                rank = t if rank is None else rank + t
            s = jnp.where(rank < TOP_K, one, zero)
            sel_in.append(s)
            contrib = s * v[k]
            gs = contrib if gs is None else gs + contrib
        gscore.append(gs)
    gsel = []
    for g in range(N_GROUPS):
        rank = None
        for m in range(N_GROUPS):
            if m == g:
                continue
            ahead = (gscore[m] > gscore[g]) if m > g else (gscore[m] >= gscore[g])
            t = jnp.where(ahead, one, zero)
            rank = t if rank is None else rank + t
        gsel.append(jnp.where(rank < 1, one, zero))
    sel = [sel_in[e] * gsel[e // epg] for e in range(N_EXPERTS)]
    denom = None
    for e in range(N_EXPERTS):
        t = sel[e] * scores[e:e + 1, :]
        denom = t if denom is None else denom + t
    gates = [sel[e] * scores[e:e + 1, :] / denom for e in range(N_EXPERTS)]
    return jnp.concatenate(gates, axis=0), jnp.concatenate(sel, axis=0)


def _ln_body(x_ref, h_ref, g_ref, b_ref, *rest, route):
    x = DN_ALPHA * x_ref[...] + h_ref[...]
    mu = jnp.mean(x, axis=-1, keepdims=True)
    xc = x - mu
    var = jnp.mean(xc * xc, axis=-1, keepdims=True)
    y = xc * lax.rsqrt(var + LN_EPS) * g_ref[...] + b_ref[...]
    if not route:
        yf_ref, yb_ref = rest
        yf_ref[...] = y
        yb_ref[...] = y.astype(BF16)
        return
    wt_ref, rb_ref, yf_ref, yb_ref, gate_ref = rest
    y_hi = y.astype(BF16)
    y_lo = (y - y_hi.astype(F32)).astype(BF16)
    yf_ref[...] = y
    yb_ref[...] = y_hi
    wt = wt_ref[...]
    w_hi = wt.astype(BF16)
    w_lo = (wt - w_hi.astype(F32)).astype(BF16)
    w_both = jnp.concatenate([w_hi, w_lo], axis=0)
    nt = (((1,), (1,)), ((), ()))
    part = (lax.dot_general(w_both, y_hi, nt, preferred_element_type=F32)
            + lax.dot_general(w_both, y_lo, nt, preferred_element_type=F32))
    logit = part[:N_EXPERTS] + part[N_EXPERTS:]
    scores = jax.nn.sigmoid(logit)
    biased = scores + rb_ref[...]
    gates, sel = _route_rows(scores, biased)
    gate_ref[...] = jnp.concatenate([gates, sel], axis=0)


def _ln_residual(x, h, g, b, router=None, name="ln"):
    M, D = x.shape
    tm = _pick(M, (256, 128, 64, 32, 16))
    row = pl.BlockSpec((tm, D), lambda i: (i, 0))
    vec = pl.BlockSpec((1, D), lambda i: (0, 0))
    in_specs = [row, row, vec, vec]
    operands = [x, h, g.reshape(1, D), b.reshape(1, D)]
    out_specs = [row, row]
    out_shape = [jax.ShapeDtypeStruct((M, D), F32), jax.ShapeDtypeStruct((M, D), BF16)]
    if router is not None:
        wt, rb = router
        in_specs += [pl.BlockSpec((N_EXPERTS, D), lambda i: (0, 0)),
                     pl.BlockSpec((N_EXPERTS, 1), lambda i: (0, 0))]
        operands += [wt, rb]
        out_specs.append(pl.BlockSpec((2 * N_EXPERTS, tm), lambda i: (0, i)))
        out_shape.append(jax.ShapeDtypeStruct((2 * N_EXPERTS, M), F32))
    return pl.pallas_call(
        functools.partial(_ln_body, route=router is not None),
        grid=(M // tm,),
        in_specs=in_specs,
        out_specs=out_specs,
        out_shape=out_shape,
        compiler_params=_params(("arbitrary",)),
        name=name,
    )(*operands)


def _moe_up_body(te_ref, nt_ref, x_ref, w1_ref, w3_ref, h_ref, w1b_ref, w3b_ref):
    i = pl.program_id(1)
    prev = te_ref[jnp.maximum(i - 1, 0)]
    fresh = jnp.logical_or(i == 0, te_ref[i] != prev)

    @pl.when(jnp.logical_and(fresh, i < nt_ref[0]))
    def _():
        w1b_ref[...] = w1_ref[...].astype(BF16)
        w3b_ref[...] = w3_ref[...].astype(BF16)

    @pl.when(i < nt_ref[0])
    def _():
        xv = x_ref[...]
        a = jnp.dot(xv, w1b_ref[...], preferred_element_type=F32)
        b = jnp.dot(xv, w3b_ref[...], preferred_element_type=F32)
        h_ref[...] = (a * jax.nn.sigmoid(a) * b).astype(BF16)

    @pl.when(i >= nt_ref[0])
    def _():
        h_ref[...] = jnp.zeros(h_ref.shape, BF16)


def _moe_down_body(te_ref, nt_ref, h_ref, w2_ref, y_ref, w2b_ref):
    i = pl.program_id(1)
    prev = te_ref[jnp.maximum(i - 1, 0)]
    fresh = jnp.logical_or(i == 0, te_ref[i] != prev)

    @pl.when(jnp.logical_and(fresh, i < nt_ref[0]))
    def _():
        w2b_ref[...] = w2_ref[...].astype(BF16)

    @pl.when(i < nt_ref[0])
    def _():
        y_ref[...] = jnp.dot(h_ref[...], w2b_ref[...], preferred_element_type=F32)

    @pl.when(i >= nt_ref[0])
    def _():
        y_ref[...] = jnp.zeros(y_ref.shape, F32)


def _moe_experts(xs, tile_expert, n_tiles, w1, w3, w2, layer):
    P, D = xs.shape
    tm = MOE_TM
    n_i = P // tm
    de = w1.shape[-1]
    tn1 = _pick(de, (512, 256, 128))
    tn2 = _pick(D, (2048, 1024, 512, 256, 128))

    def tile_row(i, nt):
        return jnp.minimum(i, nt[0] - 1)

    h = pl.pallas_call(
        _moe_up_body,
        grid_spec=pltpu.PrefetchScalarGridSpec(
            num_scalar_prefetch=2,
            grid=(de // tn1, n_i),
            in_specs=[pl.BlockSpec((tm, D), lambda j, i, te, nt: (tile_row(i, nt), 0)),
                      pl.BlockSpec((None, None, D, tn1), lambda j, i, te, nt: (layer, te[tile_row(i, nt)], 0, j)),
                      pl.BlockSpec((None, None, D, tn1), lambda j, i, te, nt: (layer, te[tile_row(i, nt)], 0, j))],
            out_specs=pl.BlockSpec((tm, tn1), lambda j, i, te, nt: (i, j)),
            scratch_shapes=[pltpu.VMEM((D, tn1), BF16), pltpu.VMEM((D, tn1), BF16)]),
        out_shape=jax.ShapeDtypeStruct((P, de), BF16),
        compiler_params=_params(("arbitrary", "arbitrary")),
        name="moe_up",
    )(tile_expert, n_tiles, xs, w1, w3)

    y = pl.pallas_call(
        _moe_down_body,
        grid_spec=pltpu.PrefetchScalarGridSpec(
            num_scalar_prefetch=2,
            grid=(D // tn2, n_i),
            in_specs=[pl.BlockSpec((tm, de), lambda j, i, te, nt: (tile_row(i, nt), 0)),
                      pl.BlockSpec((None, None, de, tn2), lambda j, i, te, nt: (layer, te[tile_row(i, nt)], 0, j))],
            out_specs=pl.BlockSpec((tm, tn2), lambda j, i, te, nt: (i, j)),
            scratch_shapes=[pltpu.VMEM((de, tn2), BF16)]),
        out_shape=jax.ShapeDtypeStruct((P, D), F32),
        compiler_params=_params(("arbitrary", "arbitrary")),
        name="moe_down",
    )(tile_expert, n_tiles, h, w2)
    return y


def _moe_layer(x1b, gate_sel, n_tok, w1, w3, w2, layer):
    Mp, D = x1b.shape
    E = N_EXPERTS
    tm = MOE_TM
    gates = gate_sel[:E, :n_tok].T
    sel = gate_sel[E:, :n_tok].T
    e0 = jnp.argmax(sel, axis=1).astype(jnp.int32)
    e1 = (E - 1 - jnp.argmax(sel[:, ::-1], axis=1)).astype(jnp.int32)
    pair_e = jnp.stack([e0, e1], axis=1)
    pair_g = jnp.take_along_axis(gates, pair_e, axis=1)
    flat_e = pair_e.reshape(-1)
    flat_t = jnp.repeat(jnp.arange(n_tok, dtype=jnp.int32), TOP_K)
    onehot = (flat_e[:, None] == jnp.arange(E, dtype=jnp.int32)[None, :]).astype(jnp.int32)
    csum = jnp.cumsum(onehot, axis=0)
    counts = csum[-1]
    rank = jnp.take_along_axis(csum, flat_e[:, None], axis=1)[:, 0] - 1
    padded = (counts + tm - 1) // tm * tm
    pend = jnp.cumsum(padded)
    pstart = pend - padded
    dest = pstart[flat_e] + rank
    P = _round_up(TOP_K * n_tok, tm) + E * tm
    row_src = jnp.zeros((P,), jnp.int32).at[dest].set(flat_t)
    n_tiles = (pend[-1] // tm).astype(jnp.int32).reshape(1)
    tile_start = jnp.arange(P // tm, dtype=jnp.int32) * tm
    tile_expert = jnp.minimum(jnp.searchsorted(pend, tile_start, side='right'), E - 1).astype(jnp.int32)
    xs = jnp.take(x1b, row_src, axis=0)
    ys = _moe_experts(xs, tile_expert, n_tiles, w1, w3, w2, layer)
    pos = jnp.concatenate([dest.reshape(n_tok, TOP_K).astype(jnp.int32),
                           jnp.zeros((Mp - n_tok, TOP_K), jnp.int32)], axis=0)
    pair_g = jnp.concatenate([pair_g, jnp.zeros((Mp - n_tok, TOP_K), F32)], axis=0)
    gate_rows = [jnp.broadcast_to(pair_g[:, k:k + 1], (Mp, LANES)) for k in range(TOP_K)]
    return ys, pos, gate_rows


def _ln_combine_body(pos_ref, pos_next_ref, x_ref, w0_ref, w1_ref, g_ref, b_ref, ys_hbm, yf_ref, yb_ref,
                     bufs, sems, *, tm):
    i = pl.program_id(0)
    n = pl.num_programs(0)
    slot = i % 2

    def issue(idx_ref, s):
        def body(r, carry):
            for k in range(TOP_K):
                pltpu.make_async_copy(ys_hbm.at[pl.ds(idx_ref[0, 0, TOP_K * r + k], 1)],
                                      bufs.at[s, k, pl.ds(r, 1)], sems.at[s, k]).start()
            return carry
        lax.fori_loop(0, tm, body, 0)

    @pl.when(i == 0)
    def _():
        issue(pos_ref, 0)

    @pl.when(i + 1 < n)
    def _():
        issue(pos_next_ref, 1 - slot)

    for k in range(TOP_K):
        pltpu.make_async_copy(ys_hbm.at[pl.ds(0, tm)], bufs.at[slot, k], sems.at[slot, k]).wait()
    buf0 = bufs.at[slot, 0]
    buf1 = bufs.at[slot, 1]
    d = x_ref.shape[1]
    f = _lanes_to(w0_ref[...], d) * buf0[...] + _lanes_to(w1_ref[...], d) * buf1[...]
    x = DN_ALPHA * x_ref[...] + f
    mu = jnp.mean(x, axis=-1, keepdims=True)
    xc = x - mu
    var = jnp.mean(xc * xc, axis=-1, keepdims=True)
    y = xc * lax.rsqrt(var + LN_EPS) * g_ref[...] + b_ref[...]
    yf_ref[...] = y
    yb_ref[...] = y.astype(BF16)


def _ln_combine(x1, ys, pos, gate_rows, g, b):
    M, D = x1.shape
    tm = _pick(M, (256, 128, 64, 32, 16))
    n = M // tm
    row = pl.BlockSpec((tm, D), lambda i: (i, 0))
    vec = pl.BlockSpec((1, D), lambda i: (0, 0))
    gcol = pl.BlockSpec((tm, LANES), lambda i: (i, 0))
    idx = pos.reshape(n, 1, TOP_K * tm)
    return pl.pallas_call(
        functools.partial(_ln_combine_body, tm=tm),
        grid=(n,),
        in_specs=[pl.BlockSpec((1, 1, TOP_K * tm), lambda i: (i, 0, 0), memory_space=pltpu.SMEM),
                  pl.BlockSpec((1, 1, TOP_K * tm), lambda i: (jnp.minimum(i + 1, n - 1), 0, 0),
                               memory_space=pltpu.SMEM),
                  row, gcol, gcol, vec, vec, pl.BlockSpec(memory_space=pl.ANY)],
        out_specs=[row, row],
        out_shape=[jax.ShapeDtypeStruct((M, D), F32), jax.ShapeDtypeStruct((M, D), BF16)],
        scratch_shapes=[pltpu.VMEM((2, TOP_K, tm, D), F32), pltpu.SemaphoreType.DMA((2, TOP_K))],
        compiler_params=_params(("arbitrary",)),
        name="ln2_combine",
    )(idx, idx, x1, gate_rows[0], gate_rows[1], g.reshape(1, D), b.reshape(1, D), ys)


def _softmax_rows(s, bias_ref, m_ref, l_ref, rs):
    tq = s.shape[0]
    m_all = m_ref[...]
    l_all = l_ref[...]
    ps, alphas, ms, ls = [], [], [], []
    for r in range(tq // rs):
        rows = slice(r * rs, (r + 1) * rs)
        sb = s[rows, :]
        if bias_ref is not None:
            sb = sb + bias_ref[rows, :]
        m_prev = m_all[rows, :]
        m_new = jnp.maximum(m_prev, jnp.max(sb, axis=1, keepdims=True))
        alpha = jnp.exp2(m_prev - m_new)
        p = jnp.exp2(sb - _lanes_to(m_new, sb.shape[1]))
        ls.append(alpha * l_all[rows, :] + jnp.sum(p, axis=1, keepdims=True))
        ms.append(m_new)
        ps.append(p.astype(BF16))
        alphas.append(alpha)
    cat = (lambda xs: xs[0]) if len(ps) == 1 else (lambda xs: jnp.concatenate(xs, axis=0))
    m_ref[...] = cat(ms)
    l_ref[...] = cat(ls)
    return cat(ps), cat(alphas)


def _lanes_to(x, width):
    return x if width == LANES else jnp.concatenate([x] * (width // LANES), axis=1)


def _flash_init(ki, state):
    @pl.when(ki == 0)
    def _():
        for m_ref, l_ref, acc_ref in state:
            m_ref[...] = jnp.full(m_ref.shape, NEG_INF, F32)
            l_ref[...] = jnp.zeros(l_ref.shape, F32)
            acc_ref[...] = jnp.zeros(acc_ref.shape, F32)


def _flash_state(scratch, n):
    return [tuple(scratch[3 * i:3 * i + 3]) for i in range(n)]


def _flash_mla_body(ki_ref, last_ref, bias_tab_ref, q_ref, k_ref, v_ref, bias_ref, o_ref, *scratch, G, rs):
    step = pl.program_id(2)
    state = _flash_state(scratch, G)
    _flash_init(ki_ref[step], state)
    nt = (((1,), (1,)), ((), ()))

    def update(bias):
        for g in range(G):
            m_ref, l_ref, acc_ref = state[g]
            q = q_ref[:, g * MLA_HEAD_PAD:(g + 1) * MLA_HEAD_PAD]
            k = k_ref[:, g * MLA_HEAD_PAD:(g + 1) * MLA_HEAD_PAD]
            s = lax.dot_general(q, k, nt, preferred_element_type=F32)
            p, alpha = _softmax_rows(s, bias, m_ref, l_ref, rs)
            pv = jnp.dot(p, v_ref[:, g * MLA_V:(g + 1) * MLA_V], preferred_element_type=F32)
            acc_ref[...] = _lanes_to(alpha, pv.shape[1]) * acc_ref[...] + pv

    pl.when(bias_tab_ref[step] == 1)(lambda: update(bias_ref))
    pl.when(bias_tab_ref[step] == 0)(lambda: update(None))

    @pl.when(last_ref[step] == 1)
    def _():
        for g in range(G):
            _, l_ref, acc_ref = state[g]
            o_ref[:, g * MLA_V:(g + 1) * MLA_V] = (acc_ref[...] / _lanes_to(l_ref[...], MLA_V)).astype(BF16)


def _flash_diff_body(ki_ref, last_ref, bias_tab_ref, q_ref, k_ref, v_ref, bias_ref, lq1_ref, lk1_ref, lq2_ref,
                     lk2_ref, gs_ref, o_ref, *scratch, G, rs, lam_init):
    step = pl.program_id(2)
    hd = 2 * LANES
    state = _flash_state(scratch, 2 * G)
    _flash_init(ki_ref[step], state)
    nt = (((1,), (1,)), ((), ()))

    def update(bias):
        for g in range(G):
            v = v_ref[:, g * hd:(g + 1) * hd]
            for c in range(2):
                m_ref, l_ref, acc_ref = state[2 * g + c]
                col = slice(g * hd + c * LANES, g * hd + (c + 1) * LANES)
                s = lax.dot_general(q_ref[:, col], k_ref[:, col], nt, preferred_element_type=F32)
                p, alpha = _softmax_rows(s, bias, m_ref, l_ref, rs)
                pv = jnp.dot(p, v, preferred_element_type=F32)
                acc_ref[...] = _lanes_to(alpha, pv.shape[1]) * acc_ref[...] + pv

    pl.when(bias_tab_ref[step] == 1)(lambda: update(bias_ref))
    pl.when(bias_tab_ref[step] == 0)(lambda: update(None))

    @pl.when(last_ref[step] == 1)
    def _():
        lam = (jnp.exp(jnp.sum(lq1_ref[...] * lk1_ref[...], axis=-1, keepdims=True))
               - jnp.exp(jnp.sum(lq2_ref[...] * lk2_ref[...], axis=-1, keepdims=True)) + lam_init)
        for g in range(G):
            (_, l0_ref, acc0_ref), (_, l1_ref, acc1_ref) = state[2 * g], state[2 * g + 1]
            o0 = acc0_ref[...] / _lanes_to(l0_ref[...], hd)
            o1 = acc1_ref[...] / _lanes_to(l1_ref[...], hd)
            o = o0 - lam * o1
            o = o * lax.rsqrt(jnp.mean(o * o, axis=-1, keepdims=True) + RMS_EPS) * gs_ref[...]
            o_ref[:, g * hd:(g + 1) * hd] = (o * (1.0 - lam_init)).astype(BF16)


def _flash(kind, q, k, v, *, n_batch, sq, sk, tq, tk, q_row0, n_heads, q_pos0, n_valid_k, extras=(),
           lam_init=0.0):
    if kind == "mla":
        dqk, dv, n_state = MLA_HEAD_PAD, MLA_V, 1
    else:
        dqk, dv, n_state = 2 * LANES, 2 * LANES, 2
    G = _pick(n_heads, (4, 2, 1))
    nq, nk = sq // tq, sk // tk

    def visible(a, b):
        return (b * tk) // CHUNK <= (q_pos0 + a * tq + tq - 1) // CHUNK and b * tk < n_valid_k

    def full(a, b):
        return (b * tk + tk - 1) // CHUNK <= (q_pos0 + a * tq) // CHUNK and b * tk + tk <= n_valid_k

    pairs = [(a, b) for a in range(nq) for b in range(nk) if visible(a, b)]
    partial = [(a, b) for a, b in pairs if not full(a, b)]
    assert partial and len({(q_pos0 + a * tq - b * tk, min(n_valid_k - b * tk, tk)) for a, b in partial}) == 1
    a0, b0 = partial[0]
    q_pos = q_pos0 + a0 * tq + lax.broadcasted_iota(jnp.int32, (tq, tk), 0)
    k_pos = b0 * tk + lax.broadcasted_iota(jnp.int32, (tq, tk), 1)
    bias = jnp.where(jnp.logical_and(k_pos // CHUNK <= q_pos // CHUNK, k_pos < n_valid_k), 0.0, NEG_INF
                     ).astype(F32)
    for n, (a, b) in enumerate(pairs):
        assert (b == 0) == (n == 0 or pairs[n - 1][0] != a)
    qi_tab = jnp.asarray([a for a, _ in pairs], jnp.int32)
    ki_tab = jnp.asarray([b for _, b in pairs], jnp.int32)
    last_tab = jnp.asarray([1 if (n + 1 == len(pairs) or pairs[n + 1][0] != a) else 0
                            for n, (a, _) in enumerate(pairs)], jnp.int32)
    bias_tab = jnp.asarray([0 if full(a, b) else 1 for a, b in pairs], jnp.int32)
    qb0 = q_row0 // tq
    assert q_row0 % tq == 0
    in_specs = [pl.BlockSpec((tq, G * dqk), lambda b, h, s, qt, kt, lt, bt: (qb0 + b * nq + qt[s], h)),
                pl.BlockSpec((tk, G * dqk), lambda b, h, s, qt, kt, lt, bt: (b * nk + kt[s], h)),
                pl.BlockSpec((tk, G * dv), lambda b, h, s, qt, kt, lt, bt: (b * nk + kt[s], h)),
                pl.BlockSpec((tq, tk), lambda b, h, s, qt, kt, lt, bt: (0, 0))]
    operands = [q, k, v, bias]
    rs = _pick(tq, (64, 32, 16))
    if kind == "mla":
        inner = functools.partial(_flash_mla_body, G=G, rs=rs)
    else:
        inner = functools.partial(_flash_diff_body, G=G, rs=rs, lam_init=lam_init)
        for e in extras:
            in_specs.append(pl.BlockSpec(e.shape, lambda b, h, s, qt, kt, lt, bt: (0, 0)))
            operands.append(e)

    def body(qi_ref, ki_ref, last_ref, bias_tab_ref, *refs):
        inner(ki_ref, last_ref, bias_tab_ref, *refs)

    return pl.pallas_call(
        body,
        grid_spec=pltpu.PrefetchScalarGridSpec(
            num_scalar_prefetch=4,
            grid=(n_batch, n_heads // G, len(pairs)),
            in_specs=in_specs,
            out_specs=pl.BlockSpec((tq, G * dv), lambda b, h, s, qt, kt, lt, bt: (b * nq + qt[s], h)),
            scratch_shapes=[pltpu.VMEM((tq, LANES), F32), pltpu.VMEM((tq, LANES), F32),
                            pltpu.VMEM((tq, dv), F32)] * (G * n_state)),
        out_shape=jax.ShapeDtypeStruct((n_batch * sq, n_heads * dv), BF16),
        compiler_params=_params(("arbitrary", "arbitrary", "arbitrary")),
        name="flash_" + kind,
    )(qi_tab, ki_tab, last_tab, bias_tab, *operands)


def _pool_body(halo_ref, x_ref, o_ref, e_ref, *, tm, tiles_per_seq, n_hist, halo):
    i = pl.program_id(0)
    first = (i % tiles_per_seq) == 0
    hv = halo_ref[...]
    if n_hist == 0:
        hv = jnp.where(first, jnp.zeros_like(hv), hv)
    e_ref[0:halo, :] = hv
    e_ref[halo:halo + tm, :] = x_ref[...]
    t_glob = (i % tiles_per_seq) * tm + lax.broadcasted_iota(jnp.int32, (tm, 1), 0)
    avail = (t_glob + 1 + n_hist).astype(F32)
    D = x_ref.shape[1]
    gw = D // len(POOL_WINDOWS)
    for g, win in enumerate(POOL_WINDOWS):
        cols = slice(g * gw, (g + 1) * gw)
        s = e_ref[halo:halo + tm, cols]
        xg = s
        for kk in range(1, win):
            s = s + e_ref[halo - kk:halo - kk + tm, cols]
        cnt = jnp.minimum(avail, jnp.float32(win))
        o_ref[:, cols] = (s / cnt - xg).astype(BF16)


def _pool_premix(x, halo_arr, *, n_tiles, tm, tiles_per_seq, n_hist, halo_index):
    D = x.shape[1]
    halo = max(POOL_WINDOWS)
    return pl.pallas_call(
        functools.partial(_pool_body, tm=tm, tiles_per_seq=tiles_per_seq, n_hist=n_hist, halo=halo),
        grid=(n_tiles,),
        in_specs=[pl.BlockSpec((halo, D), halo_index),
                  pl.BlockSpec((tm, D), lambda i: (i, 0))],
        out_specs=pl.BlockSpec((tm, D), lambda i: (i, 0)),
        out_shape=jax.ShapeDtypeStruct((n_tiles * tm, D), BF16),
        scratch_shapes=[pltpu.VMEM((halo + tm, D), F32)],
        compiler_params=_params(("arbitrary",)),
        name="pool_premix",
    )(halo_arr, x)


def _conv_body(halo_ref, u_ref, w_ref, bdw_ref, g_ref, b_ref, o_ref, e_ref, c_ref, r_refs, *, tm, tiles_per_seq,
               zero_first, halo):
    i = pl.program_id(0)
    hv = halo_ref[...]
    if zero_first:
        hv = jnp.where((i % tiles_per_seq) == 0, jnp.zeros_like(hv), hv)
    e_ref[0:halo, :] = hv
    e_ref[halo:halo + tm, :] = u_ref[...]
    D = u_ref.shape[1]
    cw = _pick(D, (256, 128))
    off = halo - (CONV_WIDTH - 1)
    sub = 8
    for cc in range(D // cw):
        cols = slice(cc * cw, (cc + 1) * cw)
        acc = None
        for phase in range(sub):
            taps = [k for k in range(CONV_WIDTH) if (off + k) % sub == phase]
            if not taps:
                continue
            first = min(off + k for k in taps)
            span = max(off + k for k in taps) - first + tm
            r_ref = r_refs.at[phase % 2]
            r_ref[0:span, :] = e_ref[first:first + span, cols]
            for k in taps:
                a = off + k - first
                term = r_ref[a:a + tm, :] * w_ref[k:k + 1, cols]
                acc = term if acc is None else acc + term
        c_ref[:, cols] = acc + bdw_ref[:, cols]
    c = c_ref[...]
    mu = jnp.mean(c, axis=-1, keepdims=True)
    cc0 = c - mu
    var = jnp.mean(cc0 * cc0, axis=-1, keepdims=True)
    y = cc0 * lax.rsqrt(var + LN_EPS) * g_ref[...] + b_ref[...]
    o_ref[...] = (y * jax.nn.sigmoid(y)).astype(BF16)


def _conv_dw(u, halo_arr, w_dw, b_dw, ln_g, ln_b, *, n_tiles, tm, tiles_per_seq, zero_first, halo_index):
    D = u.shape[1]
    halo = _round_up(CONV_WIDTH - 1, 8)
    kw = w_dw.shape[0]
    vec = pl.BlockSpec((1, D), lambda i: (0, 0))
    return pl.pallas_call(
        functools.partial(_conv_body, tm=tm, tiles_per_seq=tiles_per_seq, zero_first=zero_first, halo=halo),
        grid=(n_tiles,),
        in_specs=[pl.BlockSpec((halo, D), halo_index),
                  pl.BlockSpec((tm, D), lambda i: (i, 0)),
                  pl.BlockSpec((kw, D), lambda i: (0, 0)), vec, vec, vec],
        out_specs=pl.BlockSpec((tm, D), lambda i: (i, 0)),
        out_shape=jax.ShapeDtypeStruct((n_tiles * tm, D), BF16),
        scratch_shapes=[pltpu.VMEM((halo + tm, D), F32), pltpu.VMEM((tm, D), F32),
                        pltpu.VMEM((2, halo + tm, _pick(D, (256, 128))), F32)],
        compiler_params=_params(("arbitrary",)),
        name="conv_dw",
    )(halo_arr, u, w_dw, b_dw.reshape(1, D), ln_g.reshape(1, D), ln_b.reshape(1, D))


def _rope_tables(pos, half, width):
    inv_freq = ROPE_THETA ** (-jnp.arange(half, dtype=F32) / half)
    ang = pos.astype(F32)[:, None] * inv_freq[None, :]
    cos, sin = jnp.cos(ang), jnp.sin(ang)
    n = pos.shape[0]
    c = jnp.concatenate([cos, cos, jnp.ones((n, width - 2 * half), F32)], axis=1)
    s = jnp.concatenate([-sin, sin, jnp.zeros((n, width - 2 * half), F32)], axis=1)
    return c, s


def _tail_rows(rows, n):
    return jnp.stack([lax.slice_in_dim(rows, (b + 1) * SEQ - n, (b + 1) * SEQ, axis=0) for b in range(BATCH)])


def _stack_rows(prompt_part, sample_part, total_rows):
    pad = total_rows - prompt_part.shape[0] - sample_part.shape[0]
    return jnp.concatenate([prompt_part, sample_part,
                            jnp.zeros((pad, prompt_part.shape[1]), prompt_part.dtype)], axis=0)


def _mla_mixer(xb, pos_rows, past_c, past_pe, w_dq, g_q, w_uq, w_dkv, g_kv, w_ukv, w_o, dims):
    Tp, Ts, Mp = dims
    D = xb.shape[1]
    H = MLA_HEADS
    r = MLA_KV_RANK
    cm, sm = _rope_tables(pos_rows, MLA_ROPE // 2, LANES)

    (qlat,) = _fused_matmul(xb, w_dq, _ep_rms, tn=MLA_Q_RANK, n_cols=MLA_Q_RANK,
                            aux=[(g_q.reshape(1, -1), 'col')], outs=[(MLA_Q_RANK, BF16)], name="mla_dq")
    w_dkv_p = jnp.concatenate([w_dkv, jnp.zeros((D, LANES - MLA_ROPE), F32)], axis=1)
    c_f, c_b, pe_f, pe_b = _fused_matmul(
        xb, w_dkv_p, _ep_mla_dkv, tn=r + LANES, n_cols=r + LANES,
        aux=[(g_kv.reshape(1, r), 'full'), (cm, 'row'), (sm, 'row')],
        outs=[(r, F32), (r, BF16), (LANES, F32), (LANES, BF16)], name="mla_dkv")

    w_uq_h = w_uq.reshape(MLA_Q_RANK, H, MLA_NOPE + MLA_ROPE)
    w_uq_p = jnp.concatenate([w_uq_h, jnp.zeros((MLA_Q_RANK, H, MLA_HEAD_PAD - MLA_NOPE - MLA_ROPE), F32)],
                             axis=2).reshape(MLA_Q_RANK, H * MLA_HEAD_PAD)
    hpt = _pick(H, (4, 2, 1))
    scale = (MLA_NOPE + MLA_ROPE) ** -0.5 * LOG2E
    (q_b,) = _fused_matmul(qlat, w_uq_p, functools.partial(_ep_mla_q, scale=scale, heads_per_tile=hpt),
                           tn=hpt * MLA_HEAD_PAD, n_cols=H * MLA_HEAD_PAD,
                           aux=[(cm, 'row'), (sm, 'row')], outs=[(hpt * MLA_HEAD_PAD, BF16)], name="mla_uq")

    w_ukv_h = w_ukv.reshape(r, H, MLA_NOPE + MLA_V)
    w_uk = w_ukv_h[:, :, :MLA_NOPE].reshape(r, H * MLA_NOPE)
    w_uv = w_ukv_h[:, :, MLA_NOPE:].reshape(r, H * MLA_V)
    hpk = _pick(H, (8, 4, 2, 1))

    def kv_up(c_rows, pe_rows):
        (kf,) = _fused_matmul(c_rows, w_uk, functools.partial(_ep_mla_k, heads_per_tile=hpk),
                              tn=hpk * MLA_NOPE, n_cols=H * MLA_NOPE, aux=[(pe_rows, 'row')],
                              outs=[(hpk * MLA_HEAD_PAD, BF16)], name="mla_uk")
        (vv,) = _fused_matmul(c_rows, w_uv, _ep_plain, tn=_pick(H * MLA_V, (1024, 512, 256, 128)),
                              n_cols=H * MLA_V, outs=[(_pick(H * MLA_V, (1024, 512, 256, 128)), BF16)],
                              name="mla_uv")
        return kf, vv

    kf_p, v_p = kv_up(c_b[:Tp], pe_b[:Tp])
    tq = _pick(SEQ, (SEQ_TQ, 256, 128))
    o_p = _flash("mla", q_b, kf_p, v_p, n_batch=BATCH, sq=SEQ, sk=SEQ, tq=tq, tk=tq, q_row0=0,
                 n_heads=H, q_pos0=0, n_valid_k=SEQ)

    lk = _round_up(PAST_LEN + DEC_SEQ, LANES)
    padk = lk - PAST_LEN - DEC_SEQ
    c_s = jnp.concatenate([past_c.astype(BF16), c_b[Tp:Tp + Ts].reshape(DEC_BATCH, DEC_SEQ, r),
                           jnp.zeros((DEC_BATCH, padk, r), BF16)], axis=1).reshape(DEC_BATCH * lk, r)
    past_pe_p = jnp.concatenate([past_pe, jnp.zeros(past_pe.shape[:2] + (LANES - MLA_ROPE,), F32)], axis=2)
    pe_s = jnp.concatenate([past_pe_p.astype(BF16), pe_b[Tp:Tp + Ts].reshape(DEC_BATCH, DEC_SEQ, LANES),
                            jnp.zeros((DEC_BATCH, padk, LANES), BF16)], axis=1).reshape(DEC_BATCH * lk, LANES)
    kf_s, v_s = kv_up(c_s, pe_s)
    o_s = _flash("mla", q_b, kf_s, v_s, n_batch=DEC_BATCH, sq=DEC_SEQ, sk=lk, tq=DEC_SEQ, tk=lk, q_row0=Tp,
                 n_heads=H, q_pos0=PAST_LEN, n_valid_k=PAST_LEN + DEC_SEQ)

    o_all = _stack_rows(o_p, o_s, Mp)
    tn_o = _pick(D, (512, 256, 128))
    (h,) = _fused_matmul(o_all, w_o, _ep_plain, tn=tn_o, n_cols=D, outs=[(tn_o, F32)], name="mla_wo")
    new_p = (c_f[:Tp].reshape(BATCH, SEQ, r), pe_f[:Tp, :MLA_ROPE].reshape(BATCH, SEQ, MLA_ROPE))
    new_s = (c_f[Tp:Tp + Ts].reshape(DEC_BATCH, DEC_SEQ, r),
             pe_f[Tp:Tp + Ts, :MLA_ROPE].reshape(DEC_BATCH, DEC_SEQ, MLA_ROPE))
    return h, new_p, new_s


def _pool_mixer(x, hist, w_pool, scale, dims):
    Tp, Ts, Mp = dims
    D = x.shape[1]
    halo = max(POOL_WINDOWS)
    n_hist = halo - 1
    tm = _pick(SEQ, (POOL_TM, 128, 64, 32, 16))
    tps = SEQ // tm
    bpt = tm // halo
    d_p = _pool_premix(x, x, n_tiles=Tp // tm, tm=tm, tiles_per_seq=tps, n_hist=0,
                       halo_index=lambda i: (jnp.maximum(i * bpt - 1, 0), 0))
    xs = x[Tp:Tp + Ts].reshape(DEC_BATCH, DEC_SEQ, D)
    halo_s = jnp.concatenate([jnp.zeros((DEC_BATCH, halo - n_hist, D), F32), hist], axis=1
                             ).reshape(DEC_BATCH * halo, D)
    d_s = _pool_premix(xs.reshape(Ts, D), halo_s, n_tiles=DEC_BATCH, tm=DEC_SEQ, tiles_per_seq=1,
                       n_hist=n_hist, halo_index=lambda i: (i, 0))
    d_all = _stack_rows(d_p, d_s, Mp)
    gw = D // len(POOL_WINDOWS)
    (h,) = _fused_matmul(d_all, w_pool, _ep_scale_cols, tn=gw, n_cols=D, x_kblock=True, w_lead_from_j=True,
                         aux=[(scale.reshape(1, D), 'col')], outs=[(gw, F32)], name="pool_mm")
    new_p = _tail_rows(x, n_hist)
    new_s = jnp.concatenate([hist, xs], axis=1)[:, -n_hist:]
    return h, new_p, new_s


def _conv_mixer(x, xb, hist, w_pw1, b_pw1, w_dw, b_dw, ln_g, ln_b, w_pw2, b_pw2, dims):
    Tp, Ts, Mp = dims
    D = x.shape[1]
    n_hist = CONV_WIDTH - 1
    halo = _round_up(n_hist, 8)
    u = _glu_matmul(xb, w_pw1, b_pw1.reshape(1, -1))
    tm = _pick(SEQ, (CONV_TM, 32, 16))
    assert tm % halo == 0 or halo % tm == 0
    tps = SEQ // tm
    bpt = tm // halo
    a_p = _conv_dw(u, u, w_dw, b_dw, ln_g, ln_b, n_tiles=Tp // tm, tm=tm, tiles_per_seq=tps, zero_first=True,
                   halo_index=lambda i: (jnp.maximum(i * bpt - 1, 0), 0))
    us = u[Tp:Tp + Ts].reshape(DEC_BATCH, DEC_SEQ, D)
    halo_s = jnp.concatenate([jnp.zeros((DEC_BATCH, halo - n_hist, D), F32), hist], axis=1
                             ).reshape(DEC_BATCH * halo, D)
    a_s = _conv_dw(us.reshape(Ts, D), halo_s, w_dw, b_dw, ln_g, ln_b, n_tiles=DEC_BATCH, tm=DEC_SEQ,
                   tiles_per_seq=1, zero_first=False, halo_index=lambda i: (i, 0))
    a_all = _stack_rows(a_p, a_s, Mp)
    tn = _pick(D, (512, 256, 128))
    (h,) = _fused_matmul(a_all, w_pw2, _ep_bias, tn=tn, n_cols=D, aux=[(b_pw2.reshape(1, D), 'col')],
                         outs=[(tn, F32)], name="conv_pw2")
    new_p = _tail_rows(u, n_hist)
    new_s = jnp.concatenate([hist, us], axis=1)[:, -n_hist:]
    return h, new_p, new_s


def _diff_mixer(xb, pos_rows, past_k, past_v, w_qkv, lq1, lk1, lq2, lk2, g_sub, w_o, lam_init, dims):
    Tp, Ts, Mp = dims
    D = xb.shape[1]
    H = DIFF_HEADS
    hd = D // H
    assert hd == 2 * LANES
    cd, sd = _rope_tables(pos_rows, LANES // 2, LANES)
    scale = (hd // 2) ** -0.5 * LOG2E
    tn = _pick(D, (512, 256, 128))
    ns = tn // LANES
    tabs = [(cd, 'row'), (sd, 'row')]
    (q_b,) = _fused_matmul(xb, w_qkv, functools.partial(_ep_rope_heads, scale=scale, n_slabs=ns, want_f32=False),
                           tn=tn, n_cols=D, col0=0, aux=tabs, outs=[(tn, BF16)], name="diff_q")
    ep_k = functools.partial(_ep_rope_heads, scale=1.0, n_slabs=ns, want_f32=True)
    kv_outs = [(tn, F32), (tn, BF16)]
    kp_f, kp_b = _fused_matmul(xb, w_qkv, ep_k, tn=tn, n_cols=D, col0=D, aux=tabs, outs=kv_outs,
                               row0=0, n_rows=Tp, name="diff_k")
    ks_f, ks_b = _fused_matmul(xb, w_qkv, ep_k, tn=tn, n_cols=D, col0=D, aux=tabs, outs=kv_outs,
                               row0=Tp, n_rows=Ts, name="diff_k_s")
    vp_f, vp_b = _fused_matmul(xb, w_qkv, _ep_plain, tn=tn, n_cols=D, col0=2 * D, outs=kv_outs,
                               row0=0, n_rows=Tp, name="diff_v")
    vs_f, vs_b = _fused_matmul(xb, w_qkv, _ep_plain, tn=tn, n_cols=D, col0=2 * D, outs=kv_outs,
                               row0=Tp, n_rows=Ts, name="diff_v_s")
    extras = [lq1.reshape(1, -1), lk1.reshape(1, -1), lq2.reshape(1, -1), lk2.reshape(1, -1),
              g_sub.reshape(1, -1)]
    tq = _pick(SEQ, (SEQ_TQ, 256, 128))
    o_p = _flash("diff", q_b, kp_b, vp_b, n_batch=BATCH, sq=SEQ, sk=SEQ, tq=tq, tk=tq, q_row0=0, n_heads=H,
                 q_pos0=0, n_valid_k=SEQ, extras=extras, lam_init=lam_init)
    lk = _round_up(PAST_LEN + DEC_SEQ, LANES)
    padk = lk - PAST_LEN - DEC_SEQ

    def with_cache(past, new_b):
        return jnp.concatenate([past.reshape(DEC_BATCH, PAST_LEN, D).astype(BF16),
                                new_b.reshape(DEC_BATCH, DEC_SEQ, D),
                                jnp.zeros((DEC_BATCH, padk, D), BF16)], axis=1).reshape(DEC_BATCH * lk, D)

    o_s = _flash("diff", q_b, with_cache(past_k, ks_b), with_cache(past_v, vs_b), n_batch=DEC_BATCH, sq=DEC_SEQ,
                 sk=lk, tq=DEC_SEQ, tk=lk, q_row0=Tp, n_heads=H, q_pos0=PAST_LEN,
                 n_valid_k=PAST_LEN + DEC_SEQ, extras=extras, lam_init=lam_init)
    o_all = _stack_rows(o_p, o_s, Mp)
    (h,) = _fused_matmul(o_all, w_o, _ep_plain, tn=tn, n_cols=D, outs=[(tn, F32)], name="diff_wo")
    new_p = (kp_f.reshape(BATCH, SEQ, H, 2, hd // 2), vp_f.reshape(BATCH, SEQ, H, hd))
    new_s = (ks_f.reshape(DEC_BATCH, DEC_SEQ, H, 2, hd // 2), vs_f.reshape(DEC_BATCH, DEC_SEQ, H, hd))
    return h, new_p, new_s


def kernel(x_prompt, x_sample, cache_mla_ckv, cache_mla_kpe, state_pool, state_conv, cache_diff_k, cache_diff_v, p_prompt, p_sample, mla_w_dq, mla_g_q, mla_w_uq, mla_w_dkv, mla_g_kv, mla_w_ukv, mla_w_o, pool_w, pool_scale, conv_w_pw1, conv_b_pw1, conv_w_dw, conv_b_dw, conv_ln_g, conv_ln_b, conv_w_pw2, conv_b_pw2, diff_w_qkv, diff_lam_q1, diff_lam_k1, diff_lam_q2, diff_lam_k2, diff_g_sub, diff_w_o, router_w, router_b, moe_w1, moe_w3, moe_w2, ln1_g, ln1_b, ln2_g, ln2_b, ple_w, ple_gate):
    D = D_MODEL
    Tp, Ts = BATCH * SEQ, DEC_BATCH * DEC_SEQ
    T = Tp + Ts
    Mp = _round_up(T, ROW_PAD)
    dims = (Tp, Ts, Mp)

    x = _stack_rows(x_prompt.reshape(Tp, D), x_sample.reshape(Ts, D), Mp)
    xb = x.astype(BF16)
    p_all = jnp.concatenate([p_prompt.reshape(DEPTH, Tp, PLE_DIM), p_sample.reshape(DEPTH, Ts, PLE_DIM),
                             jnp.zeros((DEPTH, Mp - T, PLE_DIM), F32)], axis=1)
    pos_rows = jnp.concatenate([jnp.tile(jnp.arange(SEQ, dtype=jnp.int32), BATCH),
                                jnp.tile(PAST_LEN + jnp.arange(DEC_SEQ, dtype=jnp.int32), DEC_BATCH),
                                jnp.zeros((Mp - T,), jnp.int32)])

    router = (router_w.T, router_b.reshape(N_EXPERTS, 1))

    new_p = {k: [] for k in ('mla_ckv', 'mla_kpe', 'pool', 'conv', 'diff_k', 'diff_v')}
    new_s = {k: [] for k in new_p}
    for i in range(DEPTH):
        kind, j = i % N_MIXERS, i // N_MIXERS
        if kind == 0:
            h, (cp, pp), (cs, ps) = _mla_mixer(
                xb, pos_rows, cache_mla_ckv[j], cache_mla_kpe[j], mla_w_dq[j], mla_g_q[j], mla_w_uq[j],
                mla_w_dkv[j], mla_g_kv[j], mla_w_ukv[j], mla_w_o[j], dims)
            new_p['mla_ckv'].append(cp); new_p['mla_kpe'].append(pp)
            new_s['mla_ckv'].append(cs); new_s['mla_kpe'].append(ps)
        elif kind == 1:
            h, sp, ss = _pool_mixer(x, state_pool[j], pool_w[j], pool_scale[j], dims)
            new_p['pool'].append(sp); new_s['pool'].append(ss)
        elif kind == 2:
            h, sp, ss = _conv_mixer(x, xb, state_conv[j], conv_w_pw1[j], conv_b_pw1[j], conv_w_dw[j],
                                    conv_b_dw[j], conv_ln_g[j], conv_ln_b[j], conv_w_pw2[j], conv_b_pw2[j], dims)
            new_p['conv'].append(sp); new_s['conv'].append(ss)
        else:
            lam_init = 0.8 - 0.6 * math.exp(-0.3 * i)
            h, (kp, vp), (ks, vs) = _diff_mixer(
                xb, pos_rows, cache_diff_k[j], cache_diff_v[j], diff_w_qkv[j], diff_lam_q1[j], diff_lam_k1[j],
                diff_lam_q2[j], diff_lam_k2[j], diff_g_sub[j], diff_w_o[j], lam_init, dims)
            new_p['diff_k'].append(kp); new_p['diff_v'].append(vp)
            new_s['diff_k'].append(ks); new_s['diff_v'].append(vs)

        x1, x1b, gate_sel = _ln_residual(x, h, ln1_g[i], ln1_b[i], router=router, name="ln1_route")
        ys, pos, gate_rows = _moe_layer(x1b, gate_sel, T, moe_w1, moe_w3, moe_w2, i)
        x2, x2b = _ln_combine(x1, ys, pos, gate_rows, ln2_g[i], ln2_b[i])
        tn = _pick(D, (512, 256, 128))
        ple = functools.partial(_fused_matmul, x2b, ple_gate, _ep_ple, tn=tn, n_cols=D, w_lead=(i,),
                                aux=[(x2, 'tile'), (p_all[i], 'row'), (ple_w, 'colw', (i,))])
        if i + 1 < DEPTH:
            x, xb = ple(outs=[(tn, F32), (tn, BF16)], name="ple")
        else:
            (y_p,) = ple(outs=[(tn, F32)], row0=0, n_rows=Tp, name="ple_out")
            (y_s,) = ple(outs=[(tn, F32)], row0=Tp, n_rows=Ts, name="ple_out_s")

    y_prompt = y_p.reshape(BATCH, SEQ, D)
    y_sample = y_s.reshape(DEC_BATCH, DEC_SEQ, D)
    order = ('mla_ckv', 'mla_kpe', 'pool', 'conv', 'diff_k', 'diff_v')
    return (y_prompt, y_sample) + tuple(jnp.stack(new_p[k]) for k in order) \
        + tuple(jnp.stack(new_s[k]) for k in order)
```
